```python
import math
import jax, jax.numpy as jnp
from jax import lax
import numpy as np

D_MODEL = 1024
BATCH = 4
SEQ = 4096
DEPTH = 2
DEC_BATCH = 32
DEC_SEQ = 16
PAST_LEN = 1024

CHUNK = 64
MIX_WIDTH = D_MODEL
SSM_WIDTH = MIX_WIDTH // 4
SSM_GROUP = 16
SSM_GROUPS = SSM_WIDTH // SSM_GROUP
SSM_STATE = 64
DT_MIN = 0.001
DT_MAX = 0.1
HEAD_DIM = 64
ATT_WIDTH = MIX_WIDTH // 2
N_HEADS = ATT_WIDTH // HEAD_DIM
N_KV = N_HEADS // 4
Q_PER_KV = N_HEADS // N_KV
KV_WIDTH = N_KV * HEAD_DIM
WINDOW = 128
N_PREV_CHUNKS = WINDOW // CHUNK
ATT_SCALE = HEAD_DIM ** -0.5
LRU_WIDTH = MIX_WIDTH // 4
LRU_BLOCKS = 4
LRU_BLOCK = LRU_WIDTH // LRU_BLOCKS
CONV_WIDTH = 4
LRU_C = 8.0
OFF_Q = SSM_WIDTH
OFF_K = OFF_Q + ATT_WIDTH
OFF_V = OFF_K + KV_WIDTH
OFF_LX = OFF_V + KV_WIDTH
OFF_LG = OFF_LX + LRU_WIDTH
IN_WIDTH = OFF_LG + LRU_WIDTH
D_FF = 2816
EPS = 1e-6
NEG_INF = -1e30

kernel_name = 'hymba_s5_swa_rglru_streaming_step'


def _rms(x, g):
    xf = x.astype(jnp.float32)
    y = xf * lax.rsqrt(jnp.mean(xf * xf, axis=-1, keepdims=True) + EPS)
    return (y * g.astype(jnp.float32)).astype(x.dtype)


def _ffn(x, g, w_gate, w_up, w_down):
    h = _rms(x, g)
    return x + 0.5 * ((jax.nn.silu(h @ w_gate) * (h @ w_up)) @ w_down)


def _complex_combine(e1, e2):
    a1r, a1i, b1r, b1i = e1
    a2r, a2i, b2r, b2i = e2
    return (a2r * a1r - a2i * a1i,
            a2r * a1i + a2i * a1r,
            a2r * b1r - a2i * b1i + b2r,
            a2r * b1i + a2i * b1r + b2i)


def _real_combine(e1, e2):
    a1, b1 = e1
    a2, b2 = e2
    return a1 * a2, a2 * b1 + b2


def _s5(u, h0_re, h0_im, a_re, a_im, log_dt, b_re, b_im, c_re, c_im, d, w_glu):
    f32 = jnp.float32
    bn, L, _ = u.shape
    uf = u.astype(f32)
    ug = uf.reshape(bn, L, SSM_GROUPS, SSM_GROUP)
    a_re = a_re.astype(f32)
    a_im = a_im.astype(f32)
    dt = jnp.exp(log_dt.astype(f32))[:, None]
    mag = jnp.exp(a_re * dt)
    lb_re = mag * jnp.cos(a_im * dt)
    lb_im = mag * jnp.sin(a_im * dt)
    den = a_re * a_re + a_im * a_im
    nr = lb_re - 1.0
    k_re = (nr * a_re + lb_im * a_im) / den
    k_im = (lb_im * a_re - nr * a_im) / den
    b_re = b_re.astype(f32)
    b_im = b_im.astype(f32)
    bb_re = k_re[..., None] * b_re - k_im[..., None] * b_im
    bb_im = k_re[..., None] * b_im + k_im[..., None] * b_re
    x_re = jnp.einsum('gph,blgh->blgp', bb_re, ug)
    x_im = jnp.einsum('gph,blgh->blgp', bb_im, ug)
    h0_re = h0_re.astype(f32)
    h0_im = h0_im.astype(f32)
    x_re = x_re.at[:, 0].add(lb_re * h0_re - lb_im * h0_im)
    x_im = x_im.at[:, 0].add(lb_re * h0_im + lb_im * h0_re)
    a_full_re = jnp.broadcast_to(lb_re, x_re.shape)
    a_full_im = jnp.broadcast_to(lb_im, x_re.shape)
    _, _, h_re, h_im = lax.associative_scan(
        _complex_combine, (a_full_re, a_full_im, x_re, x_im), axis=1)
    y = (jnp.einsum('ghp,blgp->blgh', c_re.astype(f32), h_re)
         - jnp.einsum('ghp,blgp->blgh', c_im.astype(f32), h_im))
    y = y.reshape(bn, L, SSM_WIDTH) + d.astype(f32) * uf
    z = jax.nn.gelu(y)
    out = z * jax.nn.sigmoid(z @ w_glu.astype(f32))
    return out, h_re[:, -1], h_im[:, -1]


def _sink_softmax(s, sink):
    sk = jnp.broadcast_to(sink.astype(jnp.float32)[:, :, None, None], s.shape[:-1] + (1,))
    p = jax.nn.softmax(jnp.concatenate([s, sk], axis=-1), axis=-1)
    return p[..., :-1]


def _attn_prompt(q, k, v, sink):
    bn, L, _ = q.shape
    nc = L // CHUNK
    qc = q.reshape(bn, nc, CHUNK, N_KV, Q_PER_KV, HEAD_DIM)
    pad = ((0, 0), (N_PREV_CHUNKS, 0), (0, 0), (0, 0), (0, 0))
    kp = jnp.pad(k.reshape(bn, nc, CHUNK, N_KV, HEAD_DIM), pad)
    vp = jnp.pad(v.reshape(bn, nc, CHUNK, N_KV, HEAD_DIM), pad)
    kb = jnp.concatenate([kp[:, j:j + nc] for j in range(N_PREV_CHUNKS + 1)], axis=2)
    vb = jnp.concatenate([vp[:, j:j + nc] for j in range(N_PREV_CHUNKS + 1)], axis=2)
    chunk_ids = jnp.arange(nc)[:, None] - N_PREV_CHUNKS + jnp.arange(N_PREV_CHUNKS + 1)[None, :]
    valid = jnp.repeat(chunk_ids >= 0, CHUNK, axis=1)
    s = jnp.einsum('bcqhgd,bcshd->bchgqs', qc, kb).astype(jnp.float32) * ATT_SCALE
    s = jnp.where(valid[None, :, None, None, None, :], s, NEG_INF)
    p = _sink_softmax(s, sink)
    o = jnp.einsum('bchgqs,bcshd->bcqhgd', p.astype(vb.dtype), vb)
    return o.reshape(bn, L, ATT_WIDTH)


def _attn_sample(q, k, v, ck, cv, sink):
    bn, S, _ = q.shape
    qh = q.reshape(bn, S, N_KV, Q_PER_KV, HEAD_DIM)
    kk = jnp.concatenate([ck.astype(k.dtype), k.reshape(bn, S, N_KV, HEAD_DIM)], axis=1)
    vv = jnp.concatenate([cv.astype(v.dtype), v.reshape(bn, S, N_KV, HEAD_DIM)], axis=1)
    s = jnp.einsum('bqhgd,bshd->bhgqs', qh, kk).astype(jnp.float32) * ATT_SCALE
    p = _sink_softmax(s, sink)
    o = jnp.einsum('bhgqs,bshd->bqhgd', p.astype(vv.dtype), vv)
    return o.reshape(bn, S, ATT_WIDTH)


def _causal_conv(xb, prev, w, b):
    L = xb.shape[1]
    xp = jnp.concatenate([prev.astype(xb.dtype), xb], axis=1)
    acc = b + w[0] * xp[:, 0:L]
    for t in range(1, CONV_WIDTH):
        acc = acc + w[t] * xp[:, t:t + L]
    return acc, xp[:, -(CONV_WIDTH - 1):]


def _rg_lru(xc, h0, w_a, b_a, w_x, b_x, lam):
    f32 = jnp.float32
    bn, L, _ = xc.shape
    xf = xc.astype(f32)
    xb = xf.reshape(bn, L, LRU_BLOCKS, LRU_BLOCK)
    r = jax.nn.sigmoid(jnp.einsum('blni,nio->blno', xb, w_a.astype(f32)).reshape(bn, L, LRU_WIDTH) + b_a.astype(f32))
    i = jax.nn.sigmoid(jnp.einsum('blni,nio->blno', xb, w_x.astype(f32)).reshape(bn, L, LRU_WIDTH) + b_x.astype(f32))
    log_a = LRU_C * r * jax.nn.log_sigmoid(lam.astype(f32))
    a = jnp.exp(log_a)
    mult = jnp.sqrt(-jnp.expm1(2.0 * log_a))
    bterm = mult * i * xf
    bterm = bterm.at[:, 0].add(a[:, 0] * h0.astype(f32))
    _, h = lax.associative_scan(_real_combine, (a, bterm), axis=1)
    return h, h[:, -1]


def _mixer(h, p, st):
    f32 = jnp.float32
    bn, L, _ = h.shape
    proj = h @ p['w_in']
    u = proj[..., :OFF_Q]
    q = proj[..., OFF_Q:OFF_K]
    k = proj[..., OFF_K:OFF_V]
    v = proj[..., OFF_V:OFF_LX]
    xl = proj[..., OFF_LX:OFF_LG]
    gl = proj[..., OFF_LG:]
    sink = p['attn_sink'].reshape(N_KV, Q_PER_KV)
    if st is None:
        s_re = jnp.zeros((bn, SSM_GROUPS, SSM_STATE), f32)
        s_im = jnp.zeros((bn, SSM_GROUPS, SSM_STATE), f32)
        conv_prev = jnp.zeros((bn, CONV_WIDTH - 1, LRU_WIDTH), h.dtype)
        lru_h = jnp.zeros((bn, LRU_WIDTH), f32)
        y_att = _attn_prompt(q, k, v, sink)
        rows = min(WINDOW, L)
        new_k = k[:, L - rows:].reshape(bn, rows, N_KV, HEAD_DIM)
        new_v = v[:, L - rows:].reshape(bn, rows, N_KV, HEAD_DIM)
    else:
        ck, cv, s_re, s_im, conv_prev, lru_h = st
        y_att = _attn_sample(q, k, v, ck, cv, sink)
        new_k = k.reshape(bn, L, N_KV, HEAD_DIM)
        new_v = v.reshape(bn, L, N_KV, HEAD_DIM)
    y_ssm, n_re, n_im = _s5(u, s_re, s_im, p['ssm_a_re'], p['ssm_a_im'], p['ssm_log_dt'],
                            p['ssm_b_re'], p['ssm_b_im'], p['ssm_c_re'], p['ssm_c_im'],
                            p['ssm_d'], p['ssm_w_glu'])
    xc, new_conv = _causal_conv(xl, conv_prev, p['conv_w'], p['conv_b'])
    hl, new_lru = _rg_lru(xc, lru_h, p['lru_w_a'], p['lru_b_a'], p['lru_w_x'], p['lru_b_x'], p['lru_lambda'])
    y_lru = jax.nn.gelu(gl.astype(f32)) * hl
    g = p['out_norm']
    y = jnp.concatenate([
        _rms(y_ssm.astype(f32), g[:SSM_WIDTH]),
        _rms(y_att.astype(f32), g[SSM_WIDTH:SSM_WIDTH + ATT_WIDTH]),
        _rms(y_lru, g[SSM_WIDTH + ATT_WIDTH:]),
    ], axis=-1).astype(h.dtype)
    return y @ p['w_out'], (new_k, new_v, n_re, n_im, new_conv, new_lru)


def setup_inputs(seed: int = 0) -> dict:
    key = jax.random.key(seed)
    keys = iter(jax.random.split(key, 64))

    def nrm(shape, scale):
        return scale * jax.random.normal(next(keys), shape, jnp.float32)

    def unif(shape, lo, hi):
        return jax.random.uniform(next(keys), shape, jnp.float32, lo, hi)

    cache_len = min(WINDOW, PAST_LEN)
    n = jnp.arange(SSM_STATE, dtype=jnp.float32)
    lru_target = unif((DEPTH, LRU_WIDTH), 0.9, 0.999)
    lru_base = lru_target ** (1.0 / LRU_C)
    return {
        'x_prompt': nrm((BATCH, SEQ, D_MODEL), 1.0),
        'x_sample': nrm((DEC_BATCH, DEC_SEQ, D_MODEL), 1.0),
        'cache_k': nrm((DEPTH, DEC_BATCH, cache_len, N_KV, HEAD_DIM), 1.0),
        'cache_v': nrm((DEPTH, DEC_BATCH, cache_len, N_KV, HEAD_DIM), 1.0),
        'state_ssm_re': nrm((DEPTH, DEC_BATCH, SSM_GROUPS, SSM_STATE), 0.1),
        'state_ssm_im': nrm((DEPTH, DEC_BATCH, SSM_GROUPS, SSM_STATE), 0.1),
        'state_conv': nrm((DEPTH, DEC_BATCH, CONV_WIDTH - 1, LRU_WIDTH), 1.0),
        'state_lru': nrm((DEPTH, DEC_BATCH, LRU_WIDTH), 0.5),
        'ffn1_norm': 1.0 + nrm((DEPTH, D_MODEL), 0.01),
        'ffn1_w_gate': nrm((DEPTH, D_MODEL, D_FF), D_MODEL ** -0.5),
        'ffn1_w_up': nrm((DEPTH, D_MODEL, D_FF), D_MODEL ** -0.5),
        'ffn1_w_down': nrm((DEPTH, D_FF, D_MODEL), D_FF ** -0.5),
        'mix_norm': 1.0 + nrm((DEPTH, D_MODEL), 0.01),
        'w_in': nrm((DEPTH, D_MODEL, IN_WIDTH), D_MODEL ** -0.5),
        'ssm_a_re': -0.5 + nrm((DEPTH, SSM_GROUPS, SSM_STATE), 0.01),
        'ssm_a_im': math.pi * n + nrm((DEPTH, SSM_GROUPS, SSM_STATE), 0.01),
        'ssm_log_dt': unif((DEPTH, SSM_GROUPS), math.log(DT_MIN), math.log(DT_MAX)),
        'ssm_b_re': nrm((DEPTH, SSM_GROUPS, SSM_STATE, SSM_GROUP), (2 * SSM_GROUP) ** -0.5),
        'ssm_b_im': nrm((DEPTH, SSM_GROUPS, SSM_STATE, SSM_GROUP), (2 * SSM_GROUP) ** -0.5),
        'ssm_c_re': nrm((DEPTH, SSM_GROUPS, SSM_GROUP, SSM_STATE), SSM_STATE ** -0.5),
        'ssm_c_im': nrm((DEPTH, SSM_GROUPS, SSM_GROUP, SSM_STATE), SSM_STATE ** -0.5),
        'ssm_d': nrm((DEPTH, SSM_WIDTH), 0.5),
        'ssm_w_glu': nrm((DEPTH, SSM_WIDTH, SSM_WIDTH), SSM_WIDTH ** -0.5),
        'attn_sink': nrm((DEPTH, N_HEADS), 0.5),
        'conv_w': nrm((DEPTH, CONV_WIDTH, LRU_WIDTH), CONV_WIDTH ** -0.5),
        'conv_b': nrm((DEPTH, LRU_WIDTH), 0.01),
        'lru_w_a': nrm((DEPTH, LRU_BLOCKS, LRU_BLOCK, LRU_BLOCK), LRU_BLOCK ** -0.5),
        'lru_b_a': nrm((DEPTH, LRU_WIDTH), 0.01),
        'lru_w_x': nrm((DEPTH, LRU_BLOCKS, LRU_BLOCK, LRU_BLOCK), LRU_BLOCK ** -0.5),
        'lru_b_x': nrm((DEPTH, LRU_WIDTH), 0.01),
        'lru_lambda': jnp.log(lru_base) - jnp.log1p(-lru_base),
        'out_norm': 1.0 + nrm((DEPTH, MIX_WIDTH), 0.01),
        'w_out': nrm((DEPTH, MIX_WIDTH, D_MODEL), MIX_WIDTH ** -0.5),
        'ffn2_norm': 1.0 + nrm((DEPTH, D_MODEL), 0.01),
        'ffn2_w_gate': nrm((DEPTH, D_MODEL, D_FF), D_MODEL ** -0.5),
        'ffn2_w_up': nrm((DEPTH, D_MODEL, D_FF), D_MODEL ** -0.5),
        'ffn2_w_down': nrm((DEPTH, D_FF, D_MODEL), D_FF ** -0.5),
        'final_norm': 1.0 + nrm((D_MODEL,), 0.01),
    }


def reference(x_prompt, x_sample, cache_k, cache_v, state_ssm_re, state_ssm_im, state_conv, state_lru,
              ffn1_norm, ffn1_w_gate, ffn1_w_up, ffn1_w_down, mix_norm, w_in,
              ssm_a_re, ssm_a_im, ssm_log_dt, ssm_b_re, ssm_b_im, ssm_c_re, ssm_c_im, ssm_d, ssm_w_glu,
              attn_sink, conv_w, conv_b, lru_w_a, lru_b_a, lru_w_x, lru_b_x, lru_lambda,
              out_norm, w_out, ffn2_norm, ffn2_w_gate, ffn2_w_up, ffn2_w_down, final_norm):

    def run(x, states_in):
        collected = [[], [], [], [], [], []]
        for l in range(DEPTH):
            p = {
                'w_in': w_in[l], 'ssm_a_re': ssm_a_re[l], 'ssm_a_im': ssm_a_im[l],
                'ssm_log_dt': ssm_log_dt[l], 'ssm_b_re': ssm_b_re[l], 'ssm_b_im': ssm_b_im[l],
                'ssm_c_re': ssm_c_re[l], 'ssm_c_im': ssm_c_im[l], 'ssm_d': ssm_d[l],
                'ssm_w_glu': ssm_w_glu[l], 'attn_sink': attn_sink[l], 'conv_w': conv_w[l],
                'conv_b': conv_b[l], 'lru_w_a': lru_w_a[l], 'lru_b_a': lru_b_a[l],
                'lru_w_x': lru_w_x[l], 'lru_b_x': lru_b_x[l], 'lru_lambda': lru_lambda[l],
                'out_norm': out_norm[l], 'w_out': w_out[l],
            }
            x = _ffn(x, ffn1_norm[l], ffn1_w_gate[l], ffn1_w_up[l], ffn1_w_down[l])
            st = None if states_in is None else tuple(s[l] for s in states_in)
            m, new = _mixer(_rms(x, mix_norm[l]), p, st)
            x = x + m
            x = _ffn(x, ffn2_norm[l], ffn2_w_gate[l], ffn2_w_up[l], ffn2_w_down[l])
            for lst, s in zip(collected, new):
                lst.append(s)
        return _rms(x, final_norm), [jnp.stack(c, axis=0) for c in collected]

    y_prompt, (k_p, v_p, sre_p, sim_p, conv_p, lru_p) = run(x_prompt, None)
    y_sample, (k_s, v_s, sre_s, sim_s, conv_s, lru_s) = run(
        x_sample, (cache_k, cache_v, state_ssm_re, state_ssm_im, state_conv, state_lru))
    return (y_prompt, y_sample, k_p, v_p, sre_p, sim_p, conv_p, lru_p,
            k_s, v_s, sre_s, sim_s, conv_s, lru_s)
```

```python
import functools
import math

import jax
import jax.numpy as jnp
from jax import lax
from jax.experimental import pallas as pl
from jax.experimental.pallas import tpu as pltpu

F32 = jnp.float32
BF16 = jnp.bfloat16

D_MODEL = 1024
D_FF = 2816
DEPTH = 2
CHUNK = 64
SSM_WIDTH = 256
SSM_GROUP = 16
SSM_GROUPS = 16
SSM_STATE = 64
SSM_FLAT = SSM_GROUPS * SSM_STATE
HEAD_DIM = 64
ATT_WIDTH = 512
N_HEADS = 8
N_KV = 2
Q_PER_KV = 4
KV_WIDTH = 128
WINDOW = 128
LRU_WIDTH = 256
LRU_BLOCKS = 4
LRU_BLOCK = 64
CONV_WIDTH = 4
LRU_C = 8.0
OFF_Q = SSM_WIDTH
OFF_K = OFF_Q + ATT_WIDTH
OFF_V = OFF_K + KV_WIDTH
OFF_LX = OFF_V + KV_WIDTH
OFF_LG = OFF_LX + LRU_WIDTH
IN_WIDTH = OFF_LG + LRU_WIDTH
ATT_SCALE = HEAD_DIM ** -0.5
EPS = 1e-6
NEG_INF = -1e30

SUBLANES = 8
LANES = 128
FFN_ROWS = 512
FFN_COLS = 256
MIX_ROWS = 512
SEGS = SUBLANES
SEG_LEN = MIX_ROWS // SEGS
VMEM_LIMIT = 56 * 1024 * 1024


def _rms(x, g):
    return x * lax.rsqrt(jnp.mean(x * x, axis=-1, keepdims=True) + EPS) * g


def _dot(a, b):
    return jnp.dot(a, b, preferred_element_type=F32)


def _cmul(ar, ai, br, bi):
    return ar * br - ai * bi, ar * bi + ai * br


def _put(ref, val):
    for c in range(ref.shape[0]):
        ref[c] = val[:, c * LANES:(c + 1) * LANES]


def _get(ref):
    return jnp.concatenate([ref[c] for c in range(ref.shape[0])], axis=1)


def _get_strided(ref, start, n, stride):
    return jnp.concatenate(
        [ref[c, pl.ds(start, n, stride=stride), :] for c in range(ref.shape[0])], axis=1)


def _put_strided(ref, start, n, stride, val):
    for c in range(ref.shape[0]):
        ref[c, pl.ds(start, n, stride=stride), :] = val[:, c * LANES:(c + 1) * LANES]


def _ffn_kernel(x_ref, g_ref, wg_ref, wu_ref, wd_ref, gf_ref, o_ref, *, final_norm):
    x = x_ref[...]
    h = _rms(x, g_ref[...]).astype(BF16)
    acc = jnp.zeros(x.shape, F32)
    for c in range(D_FF // FFN_COLS):
        sl = slice(c * FFN_COLS, (c + 1) * FFN_COLS)
        g = _dot(h, wg_ref[:, sl])
        u = _dot(h, wu_ref[:, sl])
        a = (g * jax.nn.sigmoid(g) * u).astype(BF16)
        acc = acc + _dot(a, wd_ref[sl, :])
    y = x + 0.5 * acc
    if final_norm:
        y = _rms(y, gf_ref[...])
    o_ref[...] = y


def _ffn(x, g, wg, wu, wd, gf, final_norm):
    m = x.shape[0]
    const = lambda i: (0, 0)
    resident = functools.partial(pl.BlockSpec, index_map=const, pipeline_mode=pl.Buffered(1))
    return pl.pallas_call(
        functools.partial(_ffn_kernel, final_norm=final_norm),
        grid=(m // FFN_ROWS,),
        in_specs=[
            pl.BlockSpec((FFN_ROWS, D_MODEL), lambda i: (i, 0)),
            resident((1, D_MODEL)),
            resident((D_MODEL, D_FF)),
            resident((D_MODEL, D_FF)),
            resident((D_FF, D_MODEL)),
            resident((1, D_MODEL)),
        ],
        out_specs=pl.BlockSpec((FFN_ROWS, D_MODEL), lambda i: (i, 0)),
        out_shape=jax.ShapeDtypeStruct((m, D_MODEL), F32),
        compiler_params=pltpu.CompilerParams(
            dimension_semantics=("arbitrary",), vmem_limit_bytes=VMEM_LIMIT),
        name="ffn",
    )(x, g, wg, wu, wd, gf)


def _log_sigmoid(x):
    return jnp.minimum(x, 0.0) - jnp.log1p(jnp.exp(-jnp.abs(x)))


def _s5_post(y, u, d, wglu, g):
    z = jax.nn.gelu(y + d * u)
    out = z * jax.nn.sigmoid(_dot(z.astype(BF16), wglu))
    return _rms(out, g)


def _lru_gates(xc, wax, b_a, b_x, lam):
    ga = _dot(xc.astype(BF16), wax)
    r = jax.nn.sigmoid(ga[:, :LRU_WIDTH] + b_a)
    i = jax.nn.sigmoid(ga[:, LRU_WIDTH:] + b_x)
    log_a = LRU_C * r * _log_sigmoid(lam)
    a = jnp.exp(log_a)
    t = jnp.tanh(log_a)
    mult = jnp.sqrt(-2.0 * t / (1.0 - t))
    return a, mult * i * xc


def _sink_softmax(s, sink_col):
    m = jnp.maximum(jnp.max(s, axis=-1, keepdims=True), sink_col)
    e = jnp.exp(s - m)
    den = jnp.sum(e, axis=-1, keepdims=True) + jnp.exp(sink_col - m)
    return e / den


def _attend(q_rows, kk, vv, sink_ref, nq, valid):
    outs = []
    for kvh in range(N_KV):
        qs = jnp.concatenate(
            [q_rows[:, (kvh * Q_PER_KV + g) * HEAD_DIM:(kvh * Q_PER_KV + g + 1) * HEAD_DIM]
             for g in range(Q_PER_KV)], axis=0)
        kh = kk[:, kvh * HEAD_DIM:(kvh + 1) * HEAD_DIM]
        vh = vv[:, kvh * HEAD_DIM:(kvh + 1) * HEAD_DIM]
        s = lax.dot_general(qs, kh, (((1,), (1,)), ((), ())), preferred_element_type=F32) * ATT_SCALE
        if valid is not None:
            s = jnp.where(valid, s, NEG_INF)
        sink_col = jnp.concatenate(
            [jnp.full((nq, 1), sink_ref[kvh * Q_PER_KV + g], F32) for g in range(Q_PER_KV)], axis=0)
        p = _sink_softmax(s, sink_col).astype(BF16)
        o = _dot(p, vh)
        outs.extend(o[g * nq:(g + 1) * nq, :] for g in range(Q_PER_KV))
    return jnp.concatenate(outs, axis=1)


def _mix_out(x, ys, ya, yl, gout, wout):
    ycat = jnp.concatenate([
        ys,
        _rms(ya, gout[:, SSM_WIDTH:SSM_WIDTH + ATT_WIDTH]),
        _rms(yl, gout[:, SSM_WIDTH + ATT_WIDTH:]),
    ], axis=1).astype(BF16)
    return x + _dot(ycat, wout)


def _mixer_prompt_kernel(
        x_ref, gmix_ref, win_ref, lbre_ref, lbim_ref, bbig_ref, cbig_ref, d_ref, wglu_ref, sink_ref,
        convw_ref, convb_ref, wax_ref, ba_ref, bx_ref, lam_ref, gout_ref, wout_ref,
        o_ref, kout_ref, vout_ref, sre_ref, sim_ref, convout_ref, lruout_ref,
        st_re, st_im, st_lru, u_buf, up_buf, xs_buf, yp_buf, ys_buf, xl_buf, a_buf, b_buf, hl_buf,
        q_buf, k_buf, v_buf, ya_buf):
    j = pl.program_id(1)
    last = pl.num_programs(1) - 1
    T = MIX_ROWS

    @pl.when(j == 0)
    def _():
        st_re[...] = jnp.zeros_like(st_re)
        st_im[...] = jnp.zeros_like(st_im)
        st_lru[...] = jnp.zeros_like(st_lru)
        xl_buf[0:SUBLANES, :] = jnp.zeros((SUBLANES, LRU_WIDTH), F32)
        k_buf[0:WINDOW, :] = jnp.zeros((WINDOW, KV_WIDTH), BF16)
        v_buf[0:WINDOW, :] = jnp.zeros((WINDOW, KV_WIDTH), BF16)

    x = x_ref[0]
    h = _rms(x, gmix_ref[...]).astype(BF16)
    proj = _dot(h, win_ref[...])
    u = proj[:, :OFF_Q]
    k = proj[:, OFF_K:OFF_V]
    v = proj[:, OFF_V:OFF_LX]
    xl = proj[:, OFF_LX:OFF_LG]
    gl = proj[:, OFF_LG:]
    gout = gout_ref[...]

    _put(u_buf, u)
    for i in range(SEG_LEN):
        up_buf[i * SEGS:(i + 1) * SEGS, :] = _get_strided(u_buf, i, SEGS, SEG_LEN)
    up = up_buf[...]
    xs_buf[...] = _dot(up.astype(BF16), bbig_ref[...])
    lr = lbre_ref[...]
    li = lbim_ref[...]

    def s5_step(i, carry, store):
        hr, hi = carry
        r0 = pl.multiple_of(i * SEGS, SEGS)
        xr = xs_buf[pl.ds(r0, SEGS), 0:SSM_FLAT]
        xi = xs_buf[pl.ds(r0, SEGS), SSM_FLAT:2 * SSM_FLAT]
        nr = lr * hr - li * hi + xr
        ni = lr * hi + li * hr + xi
        if store:
            xs_buf[pl.ds(r0, SEGS), 0:SSM_FLAT] = nr
            xs_buf[pl.ds(r0, SEGS), SSM_FLAT:2 * SSM_FLAT] = ni
        return nr, ni

    zero_seg = jnp.zeros((SEGS, SSM_FLAT), F32)
    loc_r, loc_i = lax.fori_loop(0, SEG_LEN, functools.partial(s5_step, store=False), (zero_seg, zero_seg))
    pr, pi = lr, li
    for _ in range(int(math.log2(SEG_LEN))):
        pr, pi = _cmul(pr, pi, pr, pi)
    gr, gi = st_re[...], st_im[...]
    rows_r, rows_i = [], []
    for s in range(SEGS):
        rows_r.append(gr)
        rows_i.append(gi)
        ar, ai = _cmul(pr, pi, gr, gi)
        gr = ar + loc_r[s:s + 1, :]
        gi = ai + loc_i[s:s + 1, :]
    st_re[...] = gr
    st_im[...] = gi
    lax.fori_loop(0, SEG_LEN, functools.partial(s5_step, store=True),
                  (jnp.concatenate(rows_r, axis=0), jnp.concatenate(rows_i, axis=0)))
    y = _dot(xs_buf[...].astype(BF16), cbig_ref[...])
    yp_buf[...] = _s5_post(y, up, d_ref[...], wglu_ref[...], gout[:, :SSM_WIDTH])
    for i in range(SEG_LEN):
        _put_strided(ys_buf, i, SEGS, SEG_LEN, yp_buf[i * SEGS:(i + 1) * SEGS, :])

    xl_buf[SUBLANES:SUBLANES + T, :] = xl
    cw = convw_ref[...]
    xc = convb_ref[...] + cw[0:1, :] * xl_buf[SUBLANES - 3:SUBLANES - 3 + T, :]
    for t in range(1, CONV_WIDTH):
        xc = xc + cw[t:t + 1, :] * xl_buf[SUBLANES - 3 + t:SUBLANES - 3 + t + T, :]
    tail = xl_buf[T:T + SUBLANES, :]
    xl_buf[0:SUBLANES, :] = tail
    a, b = _lru_gates(xc, wax_ref[...], ba_ref[...], bx_ref[...], lam_ref[...])
    _put(a_buf, a)
    _put(b_buf, b)

    def lru_local(i, carry):
        pa, hh = carry
        ai_ = _get_strided(a_buf, i, SEGS, SEG_LEN)
        bi_ = _get_strided(b_buf, i, SEGS, SEG_LEN)
        return pa * ai_, ai_ * hh + bi_

    seg_a, seg_h = lax.fori_loop(
        0, SEG_LEN, lru_local, (jnp.ones((SEGS, LRU_WIDTH), F32), jnp.zeros((SEGS, LRU_WIDTH), F32)))
    gl_state = st_lru[...]
    rows = []
    for s in range(SEGS):
        rows.append(gl_state)
        gl_state = seg_a[s:s + 1, :] * gl_state + seg_h[s:s + 1, :]
    st_lru[...] = gl_state

    def lru_final(i, hh):
        ai_ = _get_strided(a_buf, i, SEGS, SEG_LEN)
        bi_ = _get_strided(b_buf, i, SEGS, SEG_LEN)
        hh = ai_ * hh + bi_
        _put_strided(hl_buf, i, SEGS, SEG_LEN, hh)
        return hh

    lax.fori_loop(0, SEG_LEN, lru_final, jnp.concatenate(rows, axis=0))
    yl = jax.nn.gelu(gl) * _get(hl_buf)

    q_buf[...] = proj[:, OFF_Q:OFF_K].astype(BF16)
    k_buf[WINDOW:WINDOW + T, :] = k.astype(BF16)
    v_buf[WINDOW:WINDOW + T, :] = v.astype(BF16)
    n_keys = WINDOW + CHUNK

    def attn_chunk(c, _):
        r0 = pl.multiple_of(c * CHUNK, CHUNK)
        kpos = r0 + lax.broadcasted_iota(jnp.int32, (1, n_keys), 1)
        valid = jnp.logical_or(kpos >= WINDOW, j > 0)
        o = _attend(q_buf[pl.ds(r0, CHUNK), :], k_buf[pl.ds(r0, n_keys), :], v_buf[pl.ds(r0, n_keys), :],
                    sink_ref, CHUNK, valid)
        ya_buf[pl.ds(r0, CHUNK), :] = o
        return 0

    lax.fori_loop(0, T // CHUNK, attn_chunk, 0)
    k_buf[0:WINDOW, :] = k_buf[T:T + WINDOW, :]
    v_buf[0:WINDOW, :] = v_buf[T:T + WINDOW, :]

    o_ref[0] = _mix_out(x, _get(ys_buf), ya_buf[...], yl, gout, wout_ref[...])

    @pl.when(j == last)
    def _():
        kout_ref[0] = k[T - WINDOW:, :]
        vout_ref[0] = v[T - WINDOW:, :]
        sre_ref[0] = gr
        sim_ref[0] = gi
        convout_ref[0] = tail[SUBLANES - (CONV_WIDTH - 1):, :]
        lruout_ref[0] = gl_state


def _mixer_prompt(x, p):
    bn, L, _ = x.shape
    T = MIX_ROWS
    const2 = lambda b, j: (0, 0)
    vspec = lambda shape: pl.BlockSpec(shape, const2)
    per_b = lambda shape: pl.BlockSpec((1,) + shape, lambda b, j: (b, 0, 0))
    in_specs = [
        pl.BlockSpec((1, T, D_MODEL), lambda b, j: (b, j, 0)),
        vspec((1, D_MODEL)), vspec((D_MODEL, IN_WIDTH)),
        vspec((1, SSM_FLAT)), vspec((1, SSM_FLAT)),
        vspec((SSM_WIDTH, 2 * SSM_FLAT)), vspec((2 * SSM_FLAT, SSM_WIDTH)),
        vspec((1, SSM_WIDTH)), vspec((SSM_WIDTH, SSM_WIDTH)),
        pl.BlockSpec(memory_space=pltpu.SMEM),
        vspec((CONV_WIDTH, LRU_WIDTH)), vspec((1, LRU_WIDTH)),
        vspec((LRU_WIDTH, 2 * LRU_WIDTH)), vspec((1, LRU_WIDTH)), vspec((1, LRU_WIDTH)), vspec((1, LRU_WIDTH)),
        vspec((1, D_MODEL)), vspec((D_MODEL, D_MODEL)),
    ]
    out_specs = [
        pl.BlockSpec((1, T, D_MODEL), lambda b, j: (b, j, 0)),
        per_b((WINDOW, KV_WIDTH)), per_b((WINDOW, KV_WIDTH)),
        per_b((1, SSM_FLAT)), per_b((1, SSM_FLAT)),
        per_b((CONV_WIDTH - 1, LRU_WIDTH)), per_b((1, LRU_WIDTH)),
    ]
    out_shape = [
        jax.ShapeDtypeStruct((bn, L, D_MODEL), F32),
        jax.ShapeDtypeStruct((bn, WINDOW, KV_WIDTH), F32), jax.ShapeDtypeStruct((bn, WINDOW, KV_WIDTH), F32),
        jax.ShapeDtypeStruct((bn, 1, SSM_FLAT), F32), jax.ShapeDtypeStruct((bn, 1, SSM_FLAT), F32),
        jax.ShapeDtypeStruct((bn, CONV_WIDTH - 1, LRU_WIDTH), F32), jax.ShapeDtypeStruct((bn, 1, LRU_WIDTH), F32),
    ]
    planes = lambda width: pltpu.VMEM((width // LANES, T, LANES), F32)
    scratch = [
        pltpu.VMEM((1, SSM_FLAT), F32), pltpu.VMEM((1, SSM_FLAT), F32), pltpu.VMEM((1, LRU_WIDTH), F32),
        planes(SSM_WIDTH), pltpu.VMEM((T, SSM_WIDTH), F32),
        pltpu.VMEM((T, 2 * SSM_FLAT), F32),
        pltpu.VMEM((T, SSM_WIDTH), F32), planes(SSM_WIDTH),
        pltpu.VMEM((T + SUBLANES, LRU_WIDTH), F32),
        planes(LRU_WIDTH), planes(LRU_WIDTH), planes(LRU_WIDTH),
        pltpu.VMEM((T, ATT_WIDTH), BF16),
        pltpu.VMEM((T + WINDOW, KV_WIDTH), BF16), pltpu.VMEM((T + WINDOW, KV_WIDTH), BF16),
        pltpu.VMEM((T, ATT_WIDTH), F32),
    ]
    return pl.pallas_call(
        _mixer_prompt_kernel,
        grid=(bn, L // T),
        in_specs=in_specs, out_specs=out_specs, out_shape=out_shape, scratch_shapes=scratch,
        compiler_params=pltpu.CompilerParams(
            dimension_semantics=("arbitrary", "arbitrary"), vmem_limit_bytes=VMEM_LIMIT),
        name="mixer_prompt",
    )(x, p["gmix"], p["win"], p["lb_re"], p["lb_im"], p["bbig"], p["cbig"], p["d"], p["wglu"], p["sink"],
      p["convw"], p["convb"], p["wax"], p["b_a"], p["b_x"], p["lam"], p["gout"], p["wout"])


def _mixer_sample_kernel(
        x_ref, ck_ref, cv_ref, s0re_ref, s0im_ref, conv0_ref, lru0_ref,
        gmix_ref, win_ref, lbre_ref, lbim_ref, bbig_ref, cbig_ref, d_ref, wglu_ref, sink_ref,
        convw_ref, convb_ref, wax_ref, ba_ref, bx_ref, lam_ref, gout_ref, wout_ref,
        o_ref, kout_ref, vout_ref, sre_ref, sim_ref, convout_ref, lruout_ref,
        xs_buf, xp_buf, a_buf, b_buf, hl_buf, q_buf, ya_buf, *, nb, ns):
    rows = nb * ns
    x = x_ref[...]
    h = _rms(x, gmix_ref[...]).astype(BF16)
    proj = _dot(h, win_ref[...])
    u = proj[:, :OFF_Q]
    k = proj[:, OFF_K:OFF_V]
    v = proj[:, OFF_V:OFF_LX]
    xl = proj[:, OFF_LX:OFF_LG]
    gl = proj[:, OFF_LG:]
    gout = gout_ref[...]
    kout_ref[...] = k
    vout_ref[...] = v

    xs_buf[...] = _dot(u.astype(BF16), bbig_ref[...])
    lr = lbre_ref[...]
    li = lbim_ref[...]
    hr = s0re_ref[...]
    hi = s0im_ref[...]
    for t in range(ns):
        sl = slice(t * nb, (t + 1) * nb)
        nr = lr * hr - li * hi + xs_buf[sl, 0:SSM_FLAT]
        ni = lr * hi + li * hr + xs_buf[sl, SSM_FLAT:2 * SSM_FLAT]
        hr, hi = nr, ni
        xs_buf[sl, 0:SSM_FLAT] = hr
        xs_buf[sl, SSM_FLAT:2 * SSM_FLAT] = hi
    sre_ref[...] = hr
    sim_ref[...] = hi
    y = _dot(xs_buf[...].astype(BF16), cbig_ref[...])
    ys = _s5_post(y, u, d_ref[...], wglu_ref[...], gout[:, :SSM_WIDTH])

    npre = (CONV_WIDTH - 1) * nb
    xp_buf[0:npre, :] = conv0_ref[...]
    xp_buf[npre:npre + rows, :] = xl
    cw = convw_ref[...]
    xc = convb_ref[...] + cw[0:1, :] * xp_buf[0:rows, :]
    for t in range(1, CONV_WIDTH):
        xc = xc + cw[t:t + 1, :] * xp_buf[t * nb:t * nb + rows, :]
    convout_ref[...] = xp_buf[rows:rows + npre, :]
    a, b = _lru_gates(xc, wax_ref[...], ba_ref[...], bx_ref[...], lam_ref[...])
    a_buf[...] = a
    b_buf[...] = b
    hh = lru0_ref[...]
    for t in range(ns):
        sl = slice(t * nb, (t + 1) * nb)
        hh = a_buf[sl, :] * hh + b_buf[sl, :]
        hl_buf[sl, :] = hh
    lruout_ref[...] = hh
    yl = jax.nn.gelu(gl) * hl_buf[...]

    _put(q_buf, proj[:, OFF_Q:OFF_K])

    def attn_stream(bi, _):
        qb = _get_strided(q_buf, bi, ns, nb).astype(BF16)
        kn = kout_ref[pl.ds(bi, ns, stride=nb), :]
        vn = vout_ref[pl.ds(bi, ns, stride=nb), :]
        kk = jnp.concatenate([ck_ref[bi], kn], axis=0).astype(BF16)
        vv = jnp.concatenate([cv_ref[bi], vn], axis=0).astype(BF16)
        _put_strided(ya_buf, bi, ns, nb, _attend(qb, kk, vv, sink_ref, ns, None))
        return 0

    lax.fori_loop(0, nb, attn_stream, 0)
    o_ref[...] = _mix_out(x, ys, _get(ya_buf), yl, gout, wout_ref[...])


def _mixer_sample(x, ck, cv, s0re, s0im, conv0, lru0, p, nb, ns):
    rows = nb * ns
    vm = pl.BlockSpec(memory_space=pltpu.VMEM)
    in_specs = [vm] * 15 + [pl.BlockSpec(memory_space=pltpu.SMEM)] + [vm] * 8
    out_shape = [
        jax.ShapeDtypeStruct((rows, D_MODEL), F32),
        jax.ShapeDtypeStruct((rows, KV_WIDTH), F32), jax.ShapeDtypeStruct((rows, KV_WIDTH), F32),
        jax.ShapeDtypeStruct((nb, SSM_FLAT), F32), jax.ShapeDtypeStruct((nb, SSM_FLAT), F32),
        jax.ShapeDtypeStruct(((CONV_WIDTH - 1) * nb, LRU_WIDTH), F32), jax.ShapeDtypeStruct((nb, LRU_WIDTH), F32),
    ]
    scratch = [
        pltpu.VMEM((rows, 2 * SSM_FLAT), F32),
        pltpu.VMEM((rows + (CONV_WIDTH - 1) * nb, LRU_WIDTH), F32),
        pltpu.VMEM((rows, LRU_WIDTH), F32), pltpu.VMEM((rows, LRU_WIDTH), F32), pltpu.VMEM((rows, LRU_WIDTH), F32),
        pltpu.VMEM((ATT_WIDTH // LANES, rows, LANES), F32),
        pltpu.VMEM((ATT_WIDTH // LANES, rows, LANES), F32),
    ]
    return pl.pallas_call(
        functools.partial(_mixer_sample_kernel, nb=nb, ns=ns),
        in_specs=in_specs, out_specs=[vm] * 7, out_shape=out_shape, scratch_shapes=scratch,
        compiler_params=pltpu.CompilerParams(vmem_limit_bytes=VMEM_LIMIT),
        name="mixer_sample",
    )(x, ck, cv, s0re, s0im, conv0, lru0,
      p["gmix"], p["win"], p["lb_re"], p["lb_im"], p["bbig"], p["cbig"], p["d"], p["wglu"], p["sink"],
      p["convw"], p["convb"], p["wax"], p["b_a"], p["b_x"], p["lam"], p["gout"], p["wout"])


def _block_diag(w):
    n, i, o = w.shape
    eye = jnp.eye(n, dtype=w.dtype)
    return jnp.einsum("nm,nio->nimo", eye, w).reshape(n * i, n * o)


def _layer_params(l, mix_norm, w_in, ssm_a_re, ssm_a_im, ssm_log_dt, ssm_b_re, ssm_b_im, ssm_c_re, ssm_c_im,
                  ssm_d, ssm_w_glu, attn_sink, conv_w, conv_b, lru_w_a, lru_b_a, lru_w_x, lru_b_x, lru_lambda,
                  out_norm, w_out):
    a_re = ssm_a_re[l]
    a_im = ssm_a_im[l]
    dt = jnp.exp(ssm_log_dt[l])[:, None]
    mag = jnp.exp(a_re * dt)
    lb_re = mag * jnp.cos(a_im * dt)
    lb_im = mag * jnp.sin(a_im * dt)
    den = a_re * a_re + a_im * a_im
    nr = lb_re - 1.0
    k_re = (nr * a_re + lb_im * a_im) / den
    k_im = (lb_im * a_re - nr * a_im) / den
    b_re = ssm_b_re[l]
    b_im = ssm_b_im[l]
    bb_re = k_re[..., None] * b_re - k_im[..., None] * b_im
    bb_im = k_re[..., None] * b_im + k_im[..., None] * b_re
    to_gh_p = lambda m: jnp.transpose(m, (0, 2, 1))
    bbig = jnp.concatenate([_block_diag(to_gh_p(bb_re)), _block_diag(to_gh_p(bb_im))], axis=1)
    to_gp_h = lambda m: jnp.transpose(m, (0, 2, 1))
    cbig = jnp.concatenate([_block_diag(to_gp_h(ssm_c_re[l])), -_block_diag(to_gp_h(ssm_c_im[l]))], axis=0)
    row = lambda vec: vec.reshape(1, -1)
    return {
        "gmix": row(mix_norm[l]), "win": w_in[l].astype(BF16),
        "lb_re": row(lb_re), "lb_im": row(lb_im),
        "bbig": bbig.astype(BF16), "cbig": cbig.astype(BF16),
        "d": row(ssm_d[l]), "wglu": ssm_w_glu[l].astype(BF16), "sink": attn_sink[l],
        "convw": conv_w[l], "convb": row(conv_b[l]),
        "wax": jnp.concatenate([_block_diag(lru_w_a[l]), _block_diag(lru_w_x[l])], axis=1).astype(BF16),
        "b_a": row(lru_b_a[l]), "b_x": row(lru_b_x[l]), "lam": row(lru_lambda[l]),
        "gout": row(out_norm[l]), "wout": w_out[l].astype(BF16),
    }


def kernel(x_prompt, x_sample, cache_k, cache_v, state_ssm_re, state_ssm_im, state_conv, state_lru,
           ffn1_norm, ffn1_w_gate, ffn1_w_up, ffn1_w_down, mix_norm, w_in,
           ssm_a_re, ssm_a_im, ssm_log_dt, ssm_b_re, ssm_b_im, ssm_c_re, ssm_c_im, ssm_d, ssm_w_glu,
           attn_sink, conv_w, conv_b, lru_w_a, lru_b_a, lru_w_x, lru_b_x, lru_lambda,
           out_norm, w_out, ffn2_norm, ffn2_w_gate, ffn2_w_up, ffn2_w_down, final_norm):
    bn, L, _ = x_prompt.shape
    nb, ns, _ = x_sample.shape
    row = lambda vec: vec.reshape(1, -1)
    gfin = row(final_norm)

    xp = x_prompt.reshape(bn * L, D_MODEL)
    xs = jnp.transpose(x_sample, (1, 0, 2)).reshape(ns * nb, D_MODEL)
    prompt_states = [[] for _ in range(6)]
    sample_states = [[] for _ in range(6)]
    for l in range(DEPTH):
        p = _layer_params(l, mix_norm, w_in, ssm_a_re, ssm_a_im, ssm_log_dt, ssm_b_re, ssm_b_im, ssm_c_re,
                          ssm_c_im, ssm_d, ssm_w_glu, attn_sink, conv_w, conv_b, lru_w_a, lru_b_a, lru_w_x,
                          lru_b_x, lru_lambda, out_norm, w_out)
        f1 = (row(ffn1_norm[l]), ffn1_w_gate[l].astype(BF16), ffn1_w_up[l].astype(BF16),
              ffn1_w_down[l].astype(BF16))
        f2 = (row(ffn2_norm[l]), ffn2_w_gate[l].astype(BF16), ffn2_w_up[l].astype(BF16),
              ffn2_w_down[l].astype(BF16))
        is_last = l == DEPTH - 1

        xp = _ffn(xp, *f1, gfin, False)
        xp3, kp, vp, srp, sip, cvp, lrp = _mixer_prompt(xp.reshape(bn, L, D_MODEL), p)
        xp = _ffn(xp3.reshape(bn * L, D_MODEL), *f2, gfin, is_last)
        for lst, s in zip(prompt_states, (
                kp.reshape(bn, WINDOW, N_KV, HEAD_DIM), vp.reshape(bn, WINDOW, N_KV, HEAD_DIM),
                srp.reshape(bn, SSM_GROUPS, SSM_STATE), sip.reshape(bn, SSM_GROUPS, SSM_STATE),
                cvp, lrp.reshape(bn, LRU_WIDTH))):
            lst.append(s)

        xs = _ffn(xs, *f1, gfin, False)
        ck = cache_k[l].reshape(nb, -1, KV_WIDTH)
        cv = cache_v[l].reshape(nb, -1, KV_WIDTH)
        conv0 = jnp.transpose(state_conv[l], (1, 0, 2)).reshape((CONV_WIDTH - 1) * nb, LRU_WIDTH)
        xs, ks, vs, srs, sis, cvs, lrs = _mixer_sample(
            xs, ck, cv, state_ssm_re[l].reshape(nb, SSM_FLAT), state_ssm_im[l].reshape(nb, SSM_FLAT),
            conv0, state_lru[l], p, nb, ns)
        xs = _ffn(xs, *f2, gfin, is_last)
        unflip = lambda m, w: jnp.transpose(m.reshape(-1, nb, w), (1, 0, 2))
        for lst, s in zip(sample_states, (
                unflip(ks, KV_WIDTH).reshape(nb, ns, N_KV, HEAD_DIM),
                unflip(vs, KV_WIDTH).reshape(nb, ns, N_KV, HEAD_DIM),
                srs.reshape(nb, SSM_GROUPS, SSM_STATE), sis.reshape(nb, SSM_GROUPS, SSM_STATE),
                unflip(cvs, LRU_WIDTH), lrs)):
            lst.append(s)

    y_prompt = xp.reshape(bn, L, D_MODEL)
    y_sample = jnp.transpose(xs.reshape(ns, nb, D_MODEL), (1, 0, 2))
    return (y_prompt, y_sample, *[jnp.stack(c, axis=0) for c in prompt_states],
            *[jnp.stack(c, axis=0) for c in sample_states])
```

```python
import functools
import math

import jax
import jax.numpy as jnp
from jax import lax
from jax.experimental import pallas as pl
from jax.experimental.pallas import tpu as pltpu

F32 = jnp.float32
BF16 = jnp.bfloat16

D_MODEL = 1024
D_FF = 2816
DEPTH = 2
CHUNK = 64
SSM_WIDTH = 256
SSM_GROUP = 16
SSM_GROUPS = 16
SSM_STATE = 64
SSM_FLAT = SSM_GROUPS * SSM_STATE
HEAD_DIM = 64
ATT_WIDTH = 512
N_HEADS = 8
N_KV = 2
Q_PER_KV = 4
KV_WIDTH = 128
WINDOW = 128
LRU_WIDTH = 256
LRU_BLOCKS = 4
LRU_BLOCK = 64
CONV_WIDTH = 4
LRU_C = 8.0
OFF_Q = SSM_WIDTH
OFF_K = OFF_Q + ATT_WIDTH
OFF_V = OFF_K + KV_WIDTH
OFF_LX = OFF_V + KV_WIDTH
OFF_LG = OFF_LX + LRU_WIDTH
IN_WIDTH = OFF_LG + LRU_WIDTH
ATT_SCALE = HEAD_DIM ** -0.5
EPS = 1e-6
NEG_INF = -1e30

SUBLANES = 8
LANES = 128
FFN_ROWS = 512
FFN_COLS = 256
MIX_ROWS = 512
SEGS = SUBLANES
SEG_LEN = MIX_ROWS // SEGS
VMEM_LIMIT = 56 * 1024 * 1024


def _rms(x, g):
    return x * lax.rsqrt(jnp.mean(x * x, axis=-1, keepdims=True) + EPS) * g


def _dot(a, b):
    return jnp.dot(a, b, preferred_element_type=F32)


def _cmul(ar, ai, br, bi):
    return ar * br - ai * bi, ar * bi + ai * br


def _put(ref, val):
    for c in range(ref.shape[0]):
        ref[c] = val[:, c * LANES:(c + 1) * LANES]


def _get(ref):
    return jnp.concatenate([ref[c] for c in range(ref.shape[0])], axis=1)


def _get_strided(ref, start, n, stride):
    return jnp.concatenate(
        [ref[c, pl.ds(start, n, stride=stride), :] for c in range(ref.shape[0])], axis=1)


def _put_strided(ref, start, n, stride, val):
    for c in range(ref.shape[0]):
        ref[c, pl.ds(start, n, stride=stride), :] = val[:, c * LANES:(c + 1) * LANES]


def _ffn_kernel(x_ref, g_ref, wg_ref, wu_ref, wd_ref, gf_ref, o_ref, *, final_norm):
    x = x_ref[...]
    h = _rms(x, g_ref[...]).astype(BF16)
    acc = jnp.zeros(x.shape, F32)
    for c in range(D_FF // FFN_COLS):
        sl = slice(c * FFN_COLS, (c + 1) * FFN_COLS)
        g = _dot(h, wg_ref[:, sl])
        u = _dot(h, wu_ref[:, sl])
        a = (g * jax.nn.sigmoid(g) * u).astype(BF16)
        acc = acc + _dot(a, wd_ref[sl, :])
    y = x + 0.5 * acc
    if final_norm:
        y = _rms(y, gf_ref[...])
    o_ref[...] = y


def _ffn(x, g, wg, wu, wd, gf, final_norm):
    m = x.shape[0]
    const = lambda i: (0, 0)
    resident = functools.partial(pl.BlockSpec, index_map=const, pipeline_mode=pl.Buffered(1))
    return pl.pallas_call(
        functools.partial(_ffn_kernel, final_norm=final_norm),
        grid=(m // FFN_ROWS,),
        in_specs=[
            pl.BlockSpec((FFN_ROWS, D_MODEL), lambda i: (i, 0)),
            resident((1, D_MODEL)),
            resident((D_MODEL, D_FF)),
            resident((D_MODEL, D_FF)),
            resident((D_FF, D_MODEL)),
            resident((1, D_MODEL)),
        ],
        out_specs=pl.BlockSpec((FFN_ROWS, D_MODEL), lambda i: (i, 0)),
        out_shape=jax.ShapeDtypeStruct((m, D_MODEL), F32),
        compiler_params=pltpu.CompilerParams(
            dimension_semantics=("arbitrary",), vmem_limit_bytes=VMEM_LIMIT),
        name="ffn",
    )(x, g, wg, wu, wd, gf)


def _log_sigmoid(x):
    return jnp.minimum(x, 0.0) - jnp.log1p(jnp.exp(-jnp.abs(x)))


def _s5_post(y, u, d, wglu, g):
    z = jax.nn.gelu(y + d * u)
    out = z * jax.nn.sigmoid(_dot(z.astype(BF16), wglu))
    return _rms(out, g)


def _lru_gates(xc, wax, b_a, b_x, lam):
    ga = _dot(xc.astype(BF16), wax)
    r = jax.nn.sigmoid(ga[:, :LRU_WIDTH] + b_a)
    i = jax.nn.sigmoid(ga[:, LRU_WIDTH:] + b_x)
    log_a = LRU_C * r * _log_sigmoid(lam)
    a = jnp.exp(log_a)
    t = jnp.tanh(log_a)
    mult = jnp.sqrt(-2.0 * t / (1.0 - t))
    return a, mult * i * xc


def _sink_softmax(s, sink_col):
    m = jnp.maximum(jnp.max(s, axis=-1, keepdims=True), sink_col)
    e = jnp.exp(s - m)
    den = jnp.sum(e, axis=-1, keepdims=True) + jnp.exp(sink_col - m)
    return e / den


def _attend(q_rows, kk, vv, sink_ref, nq, valid):
    outs = []
    for kvh in range(N_KV):
        qs = jnp.concatenate(
            [q_rows[:, (kvh * Q_PER_KV + g) * HEAD_DIM:(kvh * Q_PER_KV + g + 1) * HEAD_DIM]
             for g in range(Q_PER_KV)], axis=0)
        kh = kk[:, kvh * HEAD_DIM:(kvh + 1) * HEAD_DIM]
        vh = vv[:, kvh * HEAD_DIM:(kvh + 1) * HEAD_DIM]
        s = lax.dot_general(qs, kh, (((1,), (1,)), ((), ())), preferred_element_type=F32) * ATT_SCALE
        if valid is not None:
            s = jnp.where(valid, s, NEG_INF)
        sink_col = jnp.concatenate(
            [jnp.full((nq, 1), sink_ref[kvh * Q_PER_KV + g], F32) for g in range(Q_PER_KV)], axis=0)
        p = _sink_softmax(s, sink_col).astype(BF16)
        o = _dot(p, vh)
        outs.extend(o[g * nq:(g + 1) * nq, :] for g in range(Q_PER_KV))
    return jnp.concatenate(outs, axis=1)


def _mix_out(x, ys, ya, yl, gout, wout):
    ycat = jnp.concatenate([
        ys,
        _rms(ya, gout[:, SSM_WIDTH:SSM_WIDTH + ATT_WIDTH]),
        _rms(yl, gout[:, SSM_WIDTH + ATT_WIDTH:]),
    ], axis=1).astype(BF16)
    return x + _dot(ycat, wout)


def _mixer_prompt_kernel(
        x_ref, gmix_ref, win_ref, lbre_ref, lbim_ref, bbig_ref, cbig_ref, d_ref, wglu_ref, sink_ref,
        convw_ref, convb_ref, wax_ref, ba_ref, bx_ref, lam_ref, gout_ref, wout_ref,
        o_ref, kout_ref, vout_ref, sre_ref, sim_ref, convout_ref, lruout_ref,
        st_re, st_im, st_lru, u_buf, up_buf, xs_buf, yp_buf, ys_buf, xl_buf, a_buf, b_buf, hl_buf,
        qz_buf, k_buf, vt_buf, yt_buf):
    j = pl.program_id(1)
    last = pl.num_programs(1) - 1
    T = MIX_ROWS

    @pl.when(j == 0)
    def _():
        st_re[...] = jnp.zeros_like(st_re)
        st_im[...] = jnp.zeros_like(st_im)
        st_lru[...] = jnp.zeros_like(st_lru)
        xl_buf[0:SUBLANES, :] = jnp.zeros((SUBLANES, LRU_WIDTH), F32)
        k_buf[0:WINDOW, :] = jnp.zeros((WINDOW, KV_WIDTH), BF16)
        vt_buf[0] = jnp.zeros((KV_WIDTH, WINDOW), BF16)

    x = x_ref[0]
    h = _rms(x, gmix_ref[...]).astype(BF16)
    proj = _dot(h, win_ref[...])
    u = proj[:, :OFF_Q]
    k = proj[:, OFF_K:OFF_V]
    v = proj[:, OFF_V:OFF_LX]
    xl = proj[:, OFF_LX:OFF_LG]
    gl = proj[:, OFF_LG:]
    gout = gout_ref[...]

    _put(u_buf, u)
    for i in range(SEG_LEN):
        up_buf[i * SEGS:(i + 1) * SEGS, :] = _get_strided(u_buf, i, SEGS, SEG_LEN)
    up = up_buf[...]
    xs_buf[...] = _dot(up.astype(BF16), bbig_ref[...])
    lr = lbre_ref[...]
    li = lbim_ref[...]

    def s5_step(i, carry, store):
        hr, hi = carry
        r0 = pl.multiple_of(i * SEGS, SEGS)
        xr = xs_buf[pl.ds(r0, SEGS), 0:SSM_FLAT]
        xi = xs_buf[pl.ds(r0, SEGS), SSM_FLAT:2 * SSM_FLAT]
        nr = lr * hr - li * hi + xr
        ni = lr * hi + li * hr + xi
        if store:
            xs_buf[pl.ds(r0, SEGS), 0:SSM_FLAT] = nr
            xs_buf[pl.ds(r0, SEGS), SSM_FLAT:2 * SSM_FLAT] = ni
        return nr, ni

    zero_seg = jnp.zeros((SEGS, SSM_FLAT), F32)
    loc_r, loc_i = lax.fori_loop(0, SEG_LEN, functools.partial(s5_step, store=False), (zero_seg, zero_seg))
    pr, pi = lr, li
    for _ in range(int(math.log2(SEG_LEN))):
        pr, pi = _cmul(pr, pi, pr, pi)
    gr, gi = st_re[...], st_im[...]
    rows_r, rows_i = [], []
    for s in range(SEGS):
        rows_r.append(gr)
        rows_i.append(gi)
        ar, ai = _cmul(pr, pi, gr, gi)
        gr = ar + loc_r[s:s + 1, :]
        gi = ai + loc_i[s:s + 1, :]
    st_re[...] = gr
    st_im[...] = gi
    lax.fori_loop(0, SEG_LEN, functools.partial(s5_step, store=True),
                  (jnp.concatenate(rows_r, axis=0), jnp.concatenate(rows_i, axis=0)))
    y = _dot(xs_buf[...].astype(BF16), cbig_ref[...])
    yp_buf[...] = _s5_post(y, up, d_ref[...], wglu_ref[...], gout[:, :SSM_WIDTH])
    for i in range(SEG_LEN):
        _put_strided(ys_buf, i, SEGS, SEG_LEN, yp_buf[i * SEGS:(i + 1) * SEGS, :])

    xl_buf[SUBLANES:SUBLANES + T, :] = xl
    cw = convw_ref[...]
    xc = convb_ref[...] + cw[0:1, :] * xl_buf[SUBLANES - 3:SUBLANES - 3 + T, :]
    for t in range(1, CONV_WIDTH):
        xc = xc + cw[t:t + 1, :] * xl_buf[SUBLANES - 3 + t:SUBLANES - 3 + t + T, :]
    tail = xl_buf[T:T + SUBLANES, :]
    xl_buf[0:SUBLANES, :] = tail
    a, b = _lru_gates(xc, wax_ref[...], ba_ref[...], bx_ref[...], lam_ref[...])
    _put(a_buf, a)
    _put(b_buf, b)

    def lru_local(i, carry):
        pa, hh = carry
        ai_ = _get_strided(a_buf, i, SEGS, SEG_LEN)
        bi_ = _get_strided(b_buf, i, SEGS, SEG_LEN)
        return pa * ai_, ai_ * hh + bi_

    seg_a, seg_h = lax.fori_loop(
        0, SEG_LEN, lru_local, (jnp.ones((SEGS, LRU_WIDTH), F32), jnp.zeros((SEGS, LRU_WIDTH), F32)))
    gl_state = st_lru[...]
    rows = []
    for s in range(SEGS):
        rows.append(gl_state)
        gl_state = seg_a[s:s + 1, :] * gl_state + seg_h[s:s + 1, :]
    st_lru[...] = gl_state

    def lru_final(i, hh):
        ai_ = _get_strided(a_buf, i, SEGS, SEG_LEN)
        bi_ = _get_strided(b_buf, i, SEGS, SEG_LEN)
        hh = ai_ * hh + bi_
        _put_strided(hl_buf, i, SEGS, SEG_LEN, hh)
        return hh

    lax.fori_loop(0, SEG_LEN, lru_final, jnp.concatenate(rows, axis=0))
    yl = jax.nn.gelu(gl) * _get(hl_buf)

    lo = lax.broadcasted_iota(jnp.int32, (T, LANES), 1) < HEAD_DIM
    for jp in range(N_HEADS // 2):
        pair = proj[:, OFF_Q + jp * LANES:OFF_Q + (jp + 1) * LANES]
        swapped = pltpu.roll(pair, HEAD_DIM, axis=1)
        if jp < N_HEADS // 4:
            even, odd = jnp.where(lo, pair, 0.0), jnp.where(lo, swapped, 0.0)
        else:
            even, odd = jnp.where(lo, 0.0, swapped), jnp.where(lo, 0.0, pair)
        qz_buf[2 * jp] = even.astype(BF16)
        qz_buf[2 * jp + 1] = odd.astype(BF16)
    k_buf[WINDOW:WINDOW + T, :] = k.astype(BF16)
    vt = v.T.astype(BF16)
    for i in range(T // WINDOW):
        vt_buf[i + 1] = vt[:, i * WINDOW:(i + 1) * WINDOW]

    span = 2 * CHUNK
    krow = lax.broadcasted_iota(jnp.int32, (2 * span, 2 * span), 0)
    qcol = lax.broadcasted_iota(jnp.int32, (2 * span, 2 * span), 1) % span
    band = jnp.logical_or(jnp.logical_and(qcol < CHUNK, krow < WINDOW + CHUNK),
                          jnp.logical_and(qcol >= CHUNK, krow >= CHUNK))
    for m in range(T // span):
        valid = band if m > 0 else jnp.logical_and(band, jnp.logical_or(krow >= WINDOW, j > 0))
        kwin = k_buf[m * span:(m + 2) * span, :]
        vtwin = jnp.concatenate([vt_buf[m], vt_buf[m + 1]], axis=1)
        for hp in range(N_HEADS // 2):
            kvh = hp // (Q_PER_KV // 2)
            qz = jnp.concatenate([qz_buf[2 * hp, m * span:(m + 1) * span, :],
                                  qz_buf[2 * hp + 1, m * span:(m + 1) * span, :]], axis=0)
            s = lax.dot_general(kwin, qz, (((1,), (1,)), ((), ())), preferred_element_type=F32) * ATT_SCALE
            s = jnp.where(valid, s, NEG_INF)
            sink_row = jnp.concatenate([jnp.full((1, span), sink_ref[2 * hp], F32),
                                        jnp.full((1, span), sink_ref[2 * hp + 1], F32)], axis=1)
            mx = jnp.maximum(jnp.max(s, axis=0, keepdims=True), sink_row)
            e = jnp.exp(s - mx)
            den = jnp.sum(e, axis=0, keepdims=True) + jnp.exp(sink_row - mx)
            o = _dot(vtwin, e.astype(BF16))[kvh * HEAD_DIM:(kvh + 1) * HEAD_DIM, :] * (1.0 / den)
            yt_buf[2 * hp * HEAD_DIM:(2 * hp + 1) * HEAD_DIM, m * span:(m + 1) * span] = o[:, :span]
            yt_buf[(2 * hp + 1) * HEAD_DIM:(2 * hp + 2) * HEAD_DIM, m * span:(m + 1) * span] = o[:, span:]
    k_buf[0:WINDOW, :] = k_buf[T:T + WINDOW, :]
    vt_buf[0] = vt_buf[T // WINDOW]

    o_ref[0] = _mix_out(x, _get(ys_buf), yt_buf[...].T, yl, gout, wout_ref[...])

    @pl.when(j == last)
    def _():
        kout_ref[0] = k[T - WINDOW:, :]
        vout_ref[0] = v[T - WINDOW:, :]
        sre_ref[0] = gr
        sim_ref[0] = gi
        convout_ref[0] = tail[SUBLANES - (CONV_WIDTH - 1):, :]
        lruout_ref[0] = gl_state


def _mixer_prompt(x, p):
    bn, L, _ = x.shape
    T = MIX_ROWS
    const2 = lambda b, j: (0, 0)
    vspec = lambda shape: pl.BlockSpec(shape, const2)
    per_b = lambda shape: pl.BlockSpec((1,) + shape, lambda b, j: (b, 0, 0))
    in_specs = [
        pl.BlockSpec((1, T, D_MODEL), lambda b, j: (b, j, 0)),
        vspec((1, D_MODEL)), vspec((D_MODEL, IN_WIDTH)),
        vspec((1, SSM_FLAT)), vspec((1, SSM_FLAT)),
        vspec((SSM_WIDTH, 2 * SSM_FLAT)), vspec((2 * SSM_FLAT, SSM_WIDTH)),
        vspec((1, SSM_WIDTH)), vspec((SSM_WIDTH, SSM_WIDTH)),
        pl.BlockSpec(memory_space=pltpu.SMEM),
        vspec((CONV_WIDTH, LRU_WIDTH)), vspec((1, LRU_WIDTH)),
        vspec((LRU_WIDTH, 2 * LRU_WIDTH)), vspec((1, LRU_WIDTH)), vspec((1, LRU_WIDTH)), vspec((1, LRU_WIDTH)),
        vspec((1, D_MODEL)), vspec((D_MODEL, D_MODEL)),
    ]
    out_specs = [
        pl.BlockSpec((1, T, D_MODEL), lambda b, j: (b, j, 0)),
        per_b((WINDOW, KV_WIDTH)), per_b((WINDOW, KV_WIDTH)),
        per_b((1, SSM_FLAT)), per_b((1, SSM_FLAT)),
        per_b((CONV_WIDTH - 1, LRU_WIDTH)), per_b((1, LRU_WIDTH)),
    ]
    out_shape = [
        jax.ShapeDtypeStruct((bn, L, D_MODEL), F32),
        jax.ShapeDtypeStruct((bn, WINDOW, KV_WIDTH), F32), jax.ShapeDtypeStruct((bn, WINDOW, KV_WIDTH), F32),
        jax.ShapeDtypeStruct((bn, 1, SSM_FLAT), F32), jax.ShapeDtypeStruct((bn, 1, SSM_FLAT), F32),
        jax.ShapeDtypeStruct((bn, CONV_WIDTH - 1, LRU_WIDTH), F32), jax.ShapeDtypeStruct((bn, 1, LRU_WIDTH), F32),
    ]
    planes = lambda width: pltpu.VMEM((width // LANES, T, LANES), F32)
    scratch = [
        pltpu.VMEM((1, SSM_FLAT), F32), pltpu.VMEM((1, SSM_FLAT), F32), pltpu.VMEM((1, LRU_WIDTH), F32),
        planes(SSM_WIDTH), pltpu.VMEM((T, SSM_WIDTH), F32),
        pltpu.VMEM((T, 2 * SSM_FLAT), F32),
        pltpu.VMEM((T, SSM_WIDTH), F32), planes(SSM_WIDTH),
        pltpu.VMEM((T + SUBLANES, LRU_WIDTH), F32),
        planes(LRU_WIDTH), planes(LRU_WIDTH), planes(LRU_WIDTH),
        pltpu.VMEM((N_HEADS, T, LANES), BF16),
        pltpu.VMEM((T + WINDOW, KV_WIDTH), BF16),
        pltpu.VMEM((T // WINDOW + 1, KV_WIDTH, WINDOW), BF16),
        pltpu.VMEM((ATT_WIDTH, T), F32),
    ]
    return pl.pallas_call(
        _mixer_prompt_kernel,
        grid=(bn, L // T),
        in_specs=in_specs, out_specs=out_specs, out_shape=out_shape, scratch_shapes=scratch,
        compiler_params=pltpu.CompilerParams(
            dimension_semantics=("arbitrary", "arbitrary"), vmem_limit_bytes=VMEM_LIMIT),
        name="mixer_prompt",
    )(x, p["gmix"], p["win"], p["lb_re"], p["lb_im"], p["bbig"], p["cbig"], p["d"], p["wglu"], p["sink"],
      p["convw"], p["convb"], p["wax"], p["b_a"], p["b_x"], p["lam"], p["gout"], p["wout"])


def _mixer_sample_kernel(
        x_ref, ck_ref, cv_ref, s0re_ref, s0im_ref, conv0_ref, lru0_ref,
        gmix_ref, win_ref, lbre_ref, lbim_ref, bbig_ref, cbig_ref, d_ref, wglu_ref, sink_ref,
        convw_ref, convb_ref, wax_ref, ba_ref, bx_ref, lam_ref, gout_ref, wout_ref,
        o_ref, kout_ref, vout_ref, sre_ref, sim_ref, convout_ref, lruout_ref,
        xs_buf, xp_buf, a_buf, b_buf, hl_buf, q_buf, ya_buf, *, nb, ns):
    rows = nb * ns
    x = x_ref[...]
    h = _rms(x, gmix_ref[...]).astype(BF16)
    proj = _dot(h, win_ref[...])
    u = proj[:, :OFF_Q]
    k = proj[:, OFF_K:OFF_V]
    v = proj[:, OFF_V:OFF_LX]
    xl = proj[:, OFF_LX:OFF_LG]
    gl = proj[:, OFF_LG:]
    gout = gout_ref[...]
    kout_ref[...] = k
    vout_ref[...] = v

    xs_buf[...] = _dot(u.astype(BF16), bbig_ref[...])
    lr = lbre_ref[...]
    li = lbim_ref[...]
    hr = s0re_ref[...]
    hi = s0im_ref[...]
    for t in range(ns):
        sl = slice(t * nb, (t + 1) * nb)
        nr = lr * hr - li * hi + xs_buf[sl, 0:SSM_FLAT]
        ni = lr * hi + li * hr + xs_buf[sl, SSM_FLAT:2 * SSM_FLAT]
        hr, hi = nr, ni
        xs_buf[sl, 0:SSM_FLAT] = hr
        xs_buf[sl, SSM_FLAT:2 * SSM_FLAT] = hi
    sre_ref[...] = hr
    sim_ref[...] = hi
    y = _dot(xs_buf[...].astype(BF16), cbig_ref[...])
    ys = _s5_post(y, u, d_ref[...], wglu_ref[...], gout[:, :SSM_WIDTH])

    npre = (CONV_WIDTH - 1) * nb
    xp_buf[0:npre, :] = conv0_ref[...]
    xp_buf[npre:npre + rows, :] = xl
    cw = convw_ref[...]
    xc = convb_ref[...] + cw[0:1, :] * xp_buf[0:rows, :]
    for t in range(1, CONV_WIDTH):
        xc = xc + cw[t:t + 1, :] * xp_buf[t * nb:t * nb + rows, :]
    convout_ref[...] = xp_buf[rows:rows + npre, :]
    a, b = _lru_gates(xc, wax_ref[...], ba_ref[...], bx_ref[...], lam_ref[...])
    a_buf[...] = a
    b_buf[...] = b
    hh = lru0_ref[...]
    for t in range(ns):
        sl = slice(t * nb, (t + 1) * nb)
        hh = a_buf[sl, :] * hh + b_buf[sl, :]
        hl_buf[sl, :] = hh
    lruout_ref[...] = hh
    yl = jax.nn.gelu(gl) * hl_buf[...]

    _put(q_buf, proj[:, OFF_Q:OFF_K])

    def attn_stream(bi, _):
        qb = _get_strided(q_buf, bi, ns, nb).astype(BF16)
        kn = kout_ref[pl.ds(bi, ns, stride=nb), :]
        vn = vout_ref[pl.ds(bi, ns, stride=nb), :]
        kk = jnp.concatenate([ck_ref[bi], kn], axis=0).astype(BF16)
        vv = jnp.concatenate([cv_ref[bi], vn], axis=0).astype(BF16)
        _put_strided(ya_buf, bi, ns, nb, _attend(qb, kk, vv, sink_ref, ns, None))
        return 0

    lax.fori_loop(0, nb, attn_stream, 0)
    o_ref[...] = _mix_out(x, ys, _get(ya_buf), yl, gout, wout_ref[...])


def _mixer_sample(x, ck, cv, s0re, s0im, conv0, lru0, p, nb, ns):
    rows = nb * ns
    vm = pl.BlockSpec(memory_space=pltpu.VMEM)
    in_specs = [vm] * 15 + [pl.BlockSpec(memory_space=pltpu.SMEM)] + [vm] * 8
    out_shape = [
        jax.ShapeDtypeStruct((rows, D_MODEL), F32),
        jax.ShapeDtypeStruct((rows, KV_WIDTH), F32), jax.ShapeDtypeStruct((rows, KV_WIDTH), F32),
        jax.ShapeDtypeStruct((nb, SSM_FLAT), F32), jax.ShapeDtypeStruct((nb, SSM_FLAT), F32),
        jax.ShapeDtypeStruct(((CONV_WIDTH - 1) * nb, LRU_WIDTH), F32), jax.ShapeDtypeStruct((nb, LRU_WIDTH), F32),
    ]
    scratch = [
        pltpu.VMEM((rows, 2 * SSM_FLAT), F32),
        pltpu.VMEM((rows + (CONV_WIDTH - 1) * nb, LRU_WIDTH), F32),
        pltpu.VMEM((rows, LRU_WIDTH), F32), pltpu.VMEM((rows, LRU_WIDTH), F32), pltpu.VMEM((rows, LRU_WIDTH), F32),
        pltpu.VMEM((ATT_WIDTH // LANES, rows, LANES), F32),
        pltpu.VMEM((ATT_WIDTH // LANES, rows, LANES), F32),
    ]
    return pl.pallas_call(
        functools.partial(_mixer_sample_kernel, nb=nb, ns=ns),
        in_specs=in_specs, out_specs=[vm] * 7, out_shape=out_shape, scratch_shapes=scratch,
        compiler_params=pltpu.CompilerParams(vmem_limit_bytes=VMEM_LIMIT),
        name="mixer_sample",
    )(x, ck, cv, s0re, s0im, conv0, lru0,
      p["gmix"], p["win"], p["lb_re"], p["lb_im"], p["bbig"], p["cbig"], p["d"], p["wglu"], p["sink"],
      p["convw"], p["convb"], p["wax"], p["b_a"], p["b_x"], p["lam"], p["gout"], p["wout"])


def _block_diag(w):
    n, i, o = w.shape
    eye = jnp.eye(n, dtype=w.dtype)
    return jnp.einsum("nm,nio->nimo", eye, w).reshape(n * i, n * o)


def _layer_params(l, mix_norm, w_in, ssm_a_re, ssm_a_im, ssm_log_dt, ssm_b_re, ssm_b_im, ssm_c_re, ssm_c_im,
                  ssm_d, ssm_w_glu, attn_sink, conv_w, conv_b, lru_w_a, lru_b_a, lru_w_x, lru_b_x, lru_lambda,
                  out_norm, w_out):
    a_re = ssm_a_re[l]
    a_im = ssm_a_im[l]
    dt = jnp.exp(ssm_log_dt[l])[:, None]
    mag = jnp.exp(a_re * dt)
    lb_re = mag * jnp.cos(a_im * dt)
    lb_im = mag * jnp.sin(a_im * dt)
    den = a_re * a_re + a_im * a_im
    nr = lb_re - 1.0
    k_re = (nr * a_re + lb_im * a_im) / den
    k_im = (lb_im * a_re - nr * a_im) / den
    b_re = ssm_b_re[l]
    b_im = ssm_b_im[l]
    bb_re = k_re[..., None] * b_re - k_im[..., None] * b_im
    bb_im = k_re[..., None] * b_im + k_im[..., None] * b_re
    to_gh_p = lambda m: jnp.transpose(m, (0, 2, 1))
    bbig = jnp.concatenate([_block_diag(to_gh_p(bb_re)), _block_diag(to_gh_p(bb_im))], axis=1)
    to_gp_h = lambda m: jnp.transpose(m, (0, 2, 1))
    cbig = jnp.concatenate([_block_diag(to_gp_h(ssm_c_re[l])), -_block_diag(to_gp_h(ssm_c_im[l]))], axis=0)
    row = lambda vec: vec.reshape(1, -1)
    return {
        "gmix": row(mix_norm[l]), "win": w_in[l].astype(BF16),
        "lb_re": row(lb_re), "lb_im": row(lb_im),
        "bbig": bbig.astype(BF16), "cbig": cbig.astype(BF16),
        "d": row(ssm_d[l]), "wglu": ssm_w_glu[l].astype(BF16), "sink": attn_sink[l],
        "convw": conv_w[l], "convb": row(conv_b[l]),
        "wax": jnp.concatenate([_block_diag(lru_w_a[l]), _block_diag(lru_w_x[l])], axis=1).astype(BF16),
        "b_a": row(lru_b_a[l]), "b_x": row(lru_b_x[l]), "lam": row(lru_lambda[l]),
        "gout": row(out_norm[l]), "wout": w_out[l].astype(BF16),
    }


def kernel(x_prompt, x_sample, cache_k, cache_v, state_ssm_re, state_ssm_im, state_conv, state_lru,
           ffn1_norm, ffn1_w_gate, ffn1_w_up, ffn1_w_down, mix_norm, w_in,
           ssm_a_re, ssm_a_im, ssm_log_dt, ssm_b_re, ssm_b_im, ssm_c_re, ssm_c_im, ssm_d, ssm_w_glu,
           attn_sink, conv_w, conv_b, lru_w_a, lru_b_a, lru_w_x, lru_b_x, lru_lambda,
           out_norm, w_out, ffn2_norm, ffn2_w_gate, ffn2_w_up, ffn2_w_down, final_norm):
    bn, L, _ = x_prompt.shape
    nb, ns, _ = x_sample.shape
    row = lambda vec: vec.reshape(1, -1)
    gfin = row(final_norm)

    xp = x_prompt.reshape(bn * L, D_MODEL)
    xs = jnp.transpose(x_sample, (1, 0, 2)).reshape(ns * nb, D_MODEL)
    prompt_states = [[] for _ in range(6)]
    sample_states = [[] for _ in range(6)]
    for l in range(DEPTH):
        p = _layer_params(l, mix_norm, w_in, ssm_a_re, ssm_a_im, ssm_log_dt, ssm_b_re, ssm_b_im, ssm_c_re,
                          ssm_c_im, ssm_d, ssm_w_glu, attn_sink, conv_w, conv_b, lru_w_a, lru_b_a, lru_w_x,
                          lru_b_x, lru_lambda, out_norm, w_out)
        f1 = (row(ffn1_norm[l]), ffn1_w_gate[l].astype(BF16), ffn1_w_up[l].astype(BF16),
              ffn1_w_down[l].astype(BF16))
        f2 = (row(ffn2_norm[l]), ffn2_w_gate[l].astype(BF16), ffn2_w_up[l].astype(BF16),
              ffn2_w_down[l].astype(BF16))
        is_last = l == DEPTH - 1

        xp = _ffn(xp, *f1, gfin, False)
        xp3, kp, vp, srp, sip, cvp, lrp = _mixer_prompt(xp.reshape(bn, L, D_MODEL), p)
        xp = _ffn(xp3.reshape(bn * L, D_MODEL), *f2, gfin, is_last)
        for lst, s in zip(prompt_states, (
                kp.reshape(bn, WINDOW, N_KV, HEAD_DIM), vp.reshape(bn, WINDOW, N_KV, HEAD_DIM),
                srp.reshape(bn, SSM_GROUPS, SSM_STATE), sip.reshape(bn, SSM_GROUPS, SSM_STATE),
                cvp, lrp.reshape(bn, LRU_WIDTH))):
            lst.append(s)

        xs = _ffn(xs, *f1, gfin, False)
        ck = cache_k[l].reshape(nb, -1, KV_WIDTH)
        cv = cache_v[l].reshape(nb, -1, KV_WIDTH)
        conv0 = jnp.transpose(state_conv[l], (1, 0, 2)).reshape((CONV_WIDTH - 1) * nb, LRU_WIDTH)
        xs, ks, vs, srs, sis, cvs, lrs = _mixer_sample(
            xs, ck, cv, state_ssm_re[l].reshape(nb, SSM_FLAT), state_ssm_im[l].reshape(nb, SSM_FLAT),
            conv0, state_lru[l], p, nb, ns)
        xs = _ffn(xs, *f2, gfin, is_last)
        unflip = lambda m, w: jnp.transpose(m.reshape(-1, nb, w), (1, 0, 2))
        for lst, s in zip(sample_states, (
                unflip(ks, KV_WIDTH).reshape(nb, ns, N_KV, HEAD_DIM),
                unflip(vs, KV_WIDTH).reshape(nb, ns, N_KV, HEAD_DIM),
                srs.reshape(nb, SSM_GROUPS, SSM_STATE), sis.reshape(nb, SSM_GROUPS, SSM_STATE),
                unflip(cvs, LRU_WIDTH), lrs)):
            lst.append(s)

    y_prompt = xp.reshape(bn, L, D_MODEL)
    y_sample = jnp.transpose(xs.reshape(ns, nb, D_MODEL), (1, 0, 2))
    return (y_prompt, y_sample, *[jnp.stack(c, axis=0) for c in prompt_states],
            *[jnp.stack(c, axis=0) for c in sample_states])
```

```python
import functools
import math

import jax
import jax.numpy as jnp
from jax import lax
from jax.experimental import pallas as pl
from jax.experimental.pallas import tpu as pltpu

F32 = jnp.float32
BF16 = jnp.bfloat16

D_MODEL = 1024
D_FF = 2816
DEPTH = 2
CHUNK = 64
SSM_WIDTH = 256
SSM_GROUP = 16
SSM_GROUPS = 16
SSM_STATE = 64
SSM_FLAT = SSM_GROUPS * SSM_STATE
HEAD_DIM = 64
ATT_WIDTH = 512
N_HEADS = 8
N_KV = 2
Q_PER_KV = 4
KV_WIDTH = 128
WINDOW = 128
LRU_WIDTH = 256
LRU_BLOCKS = 4
LRU_BLOCK = 64
CONV_WIDTH = 4
LRU_C = 8.0
OFF_Q = SSM_WIDTH
OFF_K = OFF_Q + ATT_WIDTH
OFF_V = OFF_K + KV_WIDTH
OFF_LX = OFF_V + KV_WIDTH
OFF_LG = OFF_LX + LRU_WIDTH
IN_WIDTH = OFF_LG + LRU_WIDTH
ATT_SCALE = HEAD_DIM ** -0.5
EPS = 1e-6
NEG_INF = -1e30

SUBLANES = 8
LANES = 128
FFN_ROWS = 512
FFN_COLS = 256
MIX_ROWS = 512
SEGS = SUBLANES
SEG_LEN = MIX_ROWS // SEGS
VMEM_LIMIT = 56 * 1024 * 1024


def _rms(x, g):
    return x * lax.rsqrt(jnp.mean(x * x, axis=-1, keepdims=True) + EPS) * g


def _dot(a, b):
    return jnp.dot(a, b, preferred_element_type=F32)


def _cmul(ar, ai, br, bi):
    return ar * br - ai * bi, ar * bi + ai * br


def _put(ref, val):
    for c in range(ref.shape[0]):
        ref[c] = val[:, c * LANES:(c + 1) * LANES]


def _get(ref):
    return jnp.concatenate([ref[c] for c in range(ref.shape[0])], axis=1)


def _get_strided(ref, start, n, stride):
    return jnp.concatenate(
        [ref[c, pl.ds(start, n, stride=stride), :] for c in range(ref.shape[0])], axis=1)


def _put_strided(ref, start, n, stride, val):
    for c in range(ref.shape[0]):
        ref[c, pl.ds(start, n, stride=stride), :] = val[:, c * LANES:(c + 1) * LANES]


def _ffn_kernel(x_ref, g_ref, wg_ref, wu_ref, wd_ref, gf_ref, o_ref, *, final_norm):
    x = x_ref[...]
    h = _rms(x, g_ref[...]).astype(BF16)
    acc = jnp.zeros(x.shape, F32)
    for c in range(D_FF // FFN_COLS):
        sl = slice(c * FFN_COLS, (c + 1) * FFN_COLS)
        g = _dot(h, wg_ref[:, sl])
        u = _dot(h, wu_ref[:, sl])
        a = (g * jax.nn.sigmoid(g) * u).astype(BF16)
        acc = acc + _dot(a, wd_ref[sl, :])
    y = x + 0.5 * acc
    if final_norm:
        y = _rms(y, gf_ref[...])
    o_ref[...] = y


def _ffn(x, g, wg, wu, wd, gf, final_norm):
    m = x.shape[0]
    const = lambda i: (0, 0)
    resident = functools.partial(pl.BlockSpec, index_map=const, pipeline_mode=pl.Buffered(1))
    return pl.pallas_call(
        functools.partial(_ffn_kernel, final_norm=final_norm),
        grid=(m // FFN_ROWS,),
        in_specs=[
            pl.BlockSpec((FFN_ROWS, D_MODEL), lambda i: (i, 0)),
            resident((1, D_MODEL)),
            resident((D_MODEL, D_FF)),
            resident((D_MODEL, D_FF)),
            resident((D_FF, D_MODEL)),
            resident((1, D_MODEL)),
        ],
        out_specs=pl.BlockSpec((FFN_ROWS, D_MODEL), lambda i: (i, 0)),
        out_shape=jax.ShapeDtypeStruct((m, D_MODEL), F32),
        compiler_params=pltpu.CompilerParams(
            dimension_semantics=("arbitrary",), vmem_limit_bytes=VMEM_LIMIT),
        name="ffn",
    )(x, g, wg, wu, wd, gf)


def _log_sigmoid(x):
    return jnp.minimum(x, 0.0) - jnp.log1p(jnp.exp(-jnp.abs(x)))


def _s5_post(y, u, d, wglu, g):
    z = jax.nn.gelu(y + d * u)
    out = z * jax.nn.sigmoid(_dot(z.astype(BF16), wglu))
    return _rms(out, g)


def _lru_gates(xc, wax, b_a, b_x, lam):
    ga = _dot(xc.astype(BF16), wax)
    r = jax.nn.sigmoid(ga[:, :LRU_WIDTH] + b_a)
    i = jax.nn.sigmoid(ga[:, LRU_WIDTH:] + b_x)
    log_a = LRU_C * r * _log_sigmoid(lam)
    a = jnp.exp(log_a)
    t = jnp.tanh(log_a)
    mult = jnp.sqrt(-2.0 * t / (1.0 - t))
    return a, mult * i * xc


def _sink_softmax(s, sink_col):
    m = jnp.maximum(jnp.max(s, axis=-1, keepdims=True), sink_col)
    e = jnp.exp(s - m)
    den = jnp.sum(e, axis=-1, keepdims=True) + jnp.exp(sink_col - m)
    return e / den


def _attend(q_rows, kk, vv, sink_ref, nq, valid):
    outs = []
    for kvh in range(N_KV):
        qs = jnp.concatenate(
            [q_rows[:, (kvh * Q_PER_KV + g) * HEAD_DIM:(kvh * Q_PER_KV + g + 1) * HEAD_DIM]
             for g in range(Q_PER_KV)], axis=0)
        kh = kk[:, kvh * HEAD_DIM:(kvh + 1) * HEAD_DIM]
        vh = vv[:, kvh * HEAD_DIM:(kvh + 1) * HEAD_DIM]
        s = lax.dot_general(qs, kh, (((1,), (1,)), ((), ())), preferred_element_type=F32) * ATT_SCALE
        if valid is not None:
            s = jnp.where(valid, s, NEG_INF)
        sink_col = jnp.concatenate(
            [jnp.full((nq, 1), sink_ref[kvh * Q_PER_KV + g], F32) for g in range(Q_PER_KV)], axis=0)
        p = _sink_softmax(s, sink_col).astype(BF16)
        o = _dot(p, vh)
        outs.extend(o[g * nq:(g + 1) * nq, :] for g in range(Q_PER_KV))
    return jnp.concatenate(outs, axis=1)


def _mix_out(x, ys, ya, yl, gout, wout):
    ycat = jnp.concatenate([
        ys,
        _rms(ya, gout[:, SSM_WIDTH:SSM_WIDTH + ATT_WIDTH]),
        _rms(yl, gout[:, SSM_WIDTH + ATT_WIDTH:]),
    ], axis=1).astype(BF16)
    return x + _dot(ycat, wout)


def _mixer_prompt_kernel(
        x_ref, gmix_ref, win_ref, lbre_ref, lbim_ref, bbig_ref, cbig_ref, d_ref, wglu_ref, sink_ref,
        convw_ref, convb_ref, wax_ref, ba_ref, bx_ref, lam_ref, gout_ref, wout_ref,
        o_ref, kout_ref, vout_ref, sre_ref, sim_ref, convout_ref, lruout_ref,
        st_re, st_im, st_lru, u_buf, up_buf, xs_buf, yp_buf, ys_buf, xl_buf, a_buf, b_buf, hl_buf, lb_buf,
        qz_buf, k_buf, vt_buf, yt_buf):
    j = pl.program_id(1)
    last = pl.num_programs(1) - 1
    T = MIX_ROWS

    @pl.when(j == 0)
    def _():
        st_re[...] = jnp.zeros_like(st_re)
        st_im[...] = jnp.zeros_like(st_im)
        st_lru[...] = jnp.zeros_like(st_lru)
        xl_buf[0:SUBLANES, :] = jnp.zeros((SUBLANES, LRU_WIDTH), F32)
        k_buf[0:WINDOW, :] = jnp.zeros((WINDOW, KV_WIDTH), BF16)
        vt_buf[0] = jnp.zeros((KV_WIDTH, WINDOW), BF16)

    x = x_ref[0]
    h = _rms(x, gmix_ref[...]).astype(BF16)
    proj = _dot(h, win_ref[...])
    u = proj[:, :OFF_Q]
    k = proj[:, OFF_K:OFF_V]
    v = proj[:, OFF_V:OFF_LX]
    xl = proj[:, OFF_LX:OFF_LG]
    gl = proj[:, OFF_LG:]
    gout = gout_ref[...]

    _put(u_buf, u)
    for i in range(SEG_LEN):
        up_buf[i * SEGS:(i + 1) * SEGS, :] = _get_strided(u_buf, i, SEGS, SEG_LEN)
    up = up_buf[...]
    xs_buf[...] = _dot(up.astype(BF16), bbig_ref[...])
    lr = lbre_ref[...]
    li = lbim_ref[...]
    lb_buf[0] = jnp.broadcast_to(lr, (SEGS, SSM_FLAT))
    lb_buf[1] = jnp.broadcast_to(li, (SEGS, SSM_FLAT))

    def s5_step(i, carry, store):
        hr, hi = carry
        r0 = pl.multiple_of(i * SEGS, SEGS)
        xr = xs_buf[pl.ds(r0, SEGS), 0:SSM_FLAT]
        xi = xs_buf[pl.ds(r0, SEGS), SSM_FLAT:2 * SSM_FLAT]
        lrb = lb_buf[0]
        lib = lb_buf[1]
        nr = lrb * hr - lib * hi + xr
        ni = lrb * hi + lib * hr + xi
        if store:
            xs_buf[pl.ds(r0, SEGS), 0:SSM_FLAT] = nr
            xs_buf[pl.ds(r0, SEGS), SSM_FLAT:2 * SSM_FLAT] = ni
        return nr, ni

    zero_seg = jnp.zeros((SEGS, SSM_FLAT), F32)
    loc_r, loc_i = lax.fori_loop(0, SEG_LEN, functools.partial(s5_step, store=False), (zero_seg, zero_seg))
    pr, pi = lr, li
    for _ in range(int(math.log2(SEG_LEN))):
        pr, pi = _cmul(pr, pi, pr, pi)
    gr, gi = st_re[...], st_im[...]
    rows_r, rows_i = [], []
    for s in range(SEGS):
        rows_r.append(gr)
        rows_i.append(gi)
        ar, ai = _cmul(pr, pi, gr, gi)
        gr = ar + loc_r[s:s + 1, :]
        gi = ai + loc_i[s:s + 1, :]
    st_re[...] = gr
    st_im[...] = gi
    lax.fori_loop(0, SEG_LEN, functools.partial(s5_step, store=True),
                  (jnp.concatenate(rows_r, axis=0), jnp.concatenate(rows_i, axis=0)))
    y = _dot(xs_buf[...].astype(BF16), cbig_ref[...])
    yp_buf[...] = _s5_post(y, up, d_ref[...], wglu_ref[...], gout[:, :SSM_WIDTH])
    for i in range(SEG_LEN):
        _put_strided(ys_buf, i, SEGS, SEG_LEN, yp_buf[i * SEGS:(i + 1) * SEGS, :])

    xl_buf[SUBLANES:SUBLANES + T, :] = xl
    cw = convw_ref[...]
    xc = convb_ref[...] + cw[0:1, :] * xl_buf[SUBLANES - 3:SUBLANES - 3 + T, :]
    for t in range(1, CONV_WIDTH):
        xc = xc + cw[t:t + 1, :] * xl_buf[SUBLANES - 3 + t:SUBLANES - 3 + t + T, :]
    tail = xl_buf[T:T + SUBLANES, :]
    xl_buf[0:SUBLANES, :] = tail
    a, b = _lru_gates(xc, wax_ref[...], ba_ref[...], bx_ref[...], lam_ref[...])
    a3 = a.reshape(T // SUBLANES, SUBLANES, LRU_WIDTH)
    b3 = b.reshape(T // SUBLANES, SUBLANES, LRU_WIDTH)
    slab_row = lax.broadcasted_iota(jnp.int32, a3.shape, 1)
    shift = 1
    while shift < SUBLANES:
        keep = slab_row >= shift
        b3 = jnp.where(keep, a3 * pltpu.roll(b3, shift, axis=1) + b3, b3)
        a3 = jnp.where(keep, a3 * pltpu.roll(a3, shift, axis=1), a3)
        shift *= 2
    a_buf[...] = a3.reshape(T, LRU_WIDTH)
    b_buf[...] = b3.reshape(T, LRU_WIDTH)

    def lru_slab(i, carry):
        r0 = pl.multiple_of(i * SUBLANES, SUBLANES)
        hh = a_buf[pl.ds(r0, SUBLANES), :] * carry + b_buf[pl.ds(r0, SUBLANES), :]
        hl_buf[pl.ds(r0, SUBLANES), :] = hh
        return jnp.broadcast_to(hh[SUBLANES - 1:SUBLANES, :], (SUBLANES, LRU_WIDTH))

    lru_carry = lax.fori_loop(0, T // SUBLANES, lru_slab,
                              jnp.broadcast_to(st_lru[...], (SUBLANES, LRU_WIDTH)), unroll=4)
    gl_state = lru_carry[0:1, :]
    st_lru[...] = gl_state
    yl = jax.nn.gelu(gl) * hl_buf[...]

    lo = lax.broadcasted_iota(jnp.int32, (T, LANES), 1) < HEAD_DIM
    for jp in range(N_HEADS // 2):
        pair = proj[:, OFF_Q + jp * LANES:OFF_Q + (jp + 1) * LANES]
        swapped = pltpu.roll(pair, HEAD_DIM, axis=1)
        if jp < N_HEADS // 4:
            even, odd = jnp.where(lo, pair, 0.0), jnp.where(lo, swapped, 0.0)
        else:
            even, odd = jnp.where(lo, 0.0, swapped), jnp.where(lo, 0.0, pair)
        qz_buf[2 * jp] = even.astype(BF16)
        qz_buf[2 * jp + 1] = odd.astype(BF16)
    k_buf[WINDOW:WINDOW + T, :] = k.astype(BF16)
    vt = v.T.astype(BF16)
    for i in range(T // WINDOW):
        vt_buf[i + 1] = vt[:, i * WINDOW:(i + 1) * WINDOW]

    span = 2 * CHUNK
    krow = lax.broadcasted_iota(jnp.int32, (2 * span, 2 * span), 0)
    qcol = lax.broadcasted_iota(jnp.int32, (2 * span, 2 * span), 1) % span
    band = jnp.logical_or(jnp.logical_and(qcol < CHUNK, krow < WINDOW + CHUNK),
                          jnp.logical_and(qcol >= CHUNK, krow >= CHUNK))
    for m in range(T // span):
        valid = band if m > 0 else jnp.logical_and(band, jnp.logical_or(krow >= WINDOW, j > 0))
        kwin = k_buf[m * span:(m + 2) * span, :]
        vtwin = jnp.concatenate([vt_buf[m], vt_buf[m + 1]], axis=1)
        for hp in range(N_HEADS // 2):
            kvh = hp // (Q_PER_KV // 2)
            qz = jnp.concatenate([qz_buf[2 * hp, m * span:(m + 1) * span, :],
                                  qz_buf[2 * hp + 1, m * span:(m + 1) * span, :]], axis=0)
            s = lax.dot_general(kwin, qz, (((1,), (1,)), ((), ())), preferred_element_type=F32) * ATT_SCALE
            s = jnp.where(valid, s, NEG_INF)
            sink_row = jnp.concatenate([jnp.full((1, span), sink_ref[2 * hp], F32),
                                        jnp.full((1, span), sink_ref[2 * hp + 1], F32)], axis=1)
            mx = jnp.maximum(jnp.max(s, axis=0, keepdims=True), sink_row)
            e = jnp.exp(s - mx)
            den = jnp.sum(e, axis=0, keepdims=True) + jnp.exp(sink_row - mx)
            o = _dot(vtwin, e.astype(BF16))[kvh * HEAD_DIM:(kvh + 1) * HEAD_DIM, :] * (1.0 / den)
            yt_buf[2 * hp * HEAD_DIM:(2 * hp + 1) * HEAD_DIM, m * span:(m + 1) * span] = o[:, :span]
            yt_buf[(2 * hp + 1) * HEAD_DIM:(2 * hp + 2) * HEAD_DIM, m * span:(m + 1) * span] = o[:, span:]
    k_buf[0:WINDOW, :] = k_buf[T:T + WINDOW, :]
    vt_buf[0] = vt_buf[T // WINDOW]

    o_ref[0] = _mix_out(x, _get(ys_buf), yt_buf[...].T, yl, gout, wout_ref[...])

    @pl.when(j == last)
    def _():
        kout_ref[0] = k[T - WINDOW:, :]
        vout_ref[0] = v[T - WINDOW:, :]
        sre_ref[0] = gr
        sim_ref[0] = gi
        convout_ref[0] = tail[SUBLANES - (CONV_WIDTH - 1):, :]
        lruout_ref[0] = gl_state


def _mixer_prompt(x, p):
    bn, L, _ = x.shape
    T = MIX_ROWS
    const2 = lambda b, j: (0, 0)
    vspec = lambda shape: pl.BlockSpec(shape, const2)
    per_b = lambda shape: pl.BlockSpec((1,) + shape, lambda b, j: (b, 0, 0))
    in_specs = [
        pl.BlockSpec((1, T, D_MODEL), lambda b, j: (b, j, 0)),
        vspec((1, D_MODEL)), vspec((D_MODEL, IN_WIDTH)),
        vspec((1, SSM_FLAT)), vspec((1, SSM_FLAT)),
        vspec((SSM_WIDTH, 2 * SSM_FLAT)), vspec((2 * SSM_FLAT, SSM_WIDTH)),
        vspec((1, SSM_WIDTH)), vspec((SSM_WIDTH, SSM_WIDTH)),
        pl.BlockSpec(memory_space=pltpu.SMEM),
        vspec((CONV_WIDTH, LRU_WIDTH)), vspec((1, LRU_WIDTH)),
        vspec((LRU_WIDTH, 2 * LRU_WIDTH)), vspec((1, LRU_WIDTH)), vspec((1, LRU_WIDTH)), vspec((1, LRU_WIDTH)),
        vspec((1, D_MODEL)), vspec((D_MODEL, D_MODEL)),
    ]
    out_specs = [
        pl.BlockSpec((1, T, D_MODEL), lambda b, j: (b, j, 0)),
        per_b((WINDOW, KV_WIDTH)), per_b((WINDOW, KV_WIDTH)),
        per_b((1, SSM_FLAT)), per_b((1, SSM_FLAT)),
        per_b((CONV_WIDTH - 1, LRU_WIDTH)), per_b((1, LRU_WIDTH)),
    ]
    out_shape = [
        jax.ShapeDtypeStruct((bn, L, D_MODEL), F32),
        jax.ShapeDtypeStruct((bn, WINDOW, KV_WIDTH), F32), jax.ShapeDtypeStruct((bn, WINDOW, KV_WIDTH), F32),
        jax.ShapeDtypeStruct((bn, 1, SSM_FLAT), F32), jax.ShapeDtypeStruct((bn, 1, SSM_FLAT), F32),
        jax.ShapeDtypeStruct((bn, CONV_WIDTH - 1, LRU_WIDTH), F32), jax.ShapeDtypeStruct((bn, 1, LRU_WIDTH), F32),
    ]
    planes = lambda width: pltpu.VMEM((width // LANES, T, LANES), F32)
    scratch = [
        pltpu.VMEM((1, SSM_FLAT), F32), pltpu.VMEM((1, SSM_FLAT), F32), pltpu.VMEM((1, LRU_WIDTH), F32),
        planes(SSM_WIDTH), pltpu.VMEM((T, SSM_WIDTH), F32),
        pltpu.VMEM((T, 2 * SSM_FLAT), F32),
        pltpu.VMEM((T, SSM_WIDTH), F32), planes(SSM_WIDTH),
        pltpu.VMEM((T + SUBLANES, LRU_WIDTH), F32),
        pltpu.VMEM((T, LRU_WIDTH), F32), pltpu.VMEM((T, LRU_WIDTH), F32), pltpu.VMEM((T, LRU_WIDTH), F32),
        pltpu.VMEM((2, SEGS, SSM_FLAT), F32),
        pltpu.VMEM((N_HEADS, T, LANES), BF16),
        pltpu.VMEM((T + WINDOW, KV_WIDTH), BF16),
        pltpu.VMEM((T // WINDOW + 1, KV_WIDTH, WINDOW), BF16),
        pltpu.VMEM((ATT_WIDTH, T), F32),
    ]
    return pl.pallas_call(
        _mixer_prompt_kernel,
        grid=(bn, L // T),
        in_specs=in_specs, out_specs=out_specs, out_shape=out_shape, scratch_shapes=scratch,
        compiler_params=pltpu.CompilerParams(
            dimension_semantics=("arbitrary", "arbitrary"), vmem_limit_bytes=VMEM_LIMIT),
        name="mixer_prompt",
    )(x, p["gmix"], p["win"], p["lb_re"], p["lb_im"], p["bbig"], p["cbig"], p["d"], p["wglu"], p["sink"],
      p["convw"], p["convb"], p["wax"], p["b_a"], p["b_x"], p["lam"], p["gout"], p["wout"])


def _mixer_sample_kernel(
        x_ref, ck_ref, cv_ref, s0re_ref, s0im_ref, conv0_ref, lru0_ref,
        gmix_ref, win_ref, lbre_ref, lbim_ref, bbig_ref, cbig_ref, d_ref, wglu_ref, sink_ref,
        convw_ref, convb_ref, wax_ref, ba_ref, bx_ref, lam_ref, gout_ref, wout_ref,
        o_ref, kout_ref, vout_ref, sre_ref, sim_ref, convout_ref, lruout_ref,
        xs_buf, xp_buf, a_buf, b_buf, hl_buf, q_buf, ya_buf, *, nb, ns):
    rows = nb * ns
    x = x_ref[...]
    h = _rms(x, gmix_ref[...]).astype(BF16)
    proj = _dot(h, win_ref[...])
    u = proj[:, :OFF_Q]
    k = proj[:, OFF_K:OFF_V]
    v = proj[:, OFF_V:OFF_LX]
    xl = proj[:, OFF_LX:OFF_LG]
    gl = proj[:, OFF_LG:]
    gout = gout_ref[...]
    kout_ref[...] = k
    vout_ref[...] = v

    xs_buf[...] = _dot(u.astype(BF16), bbig_ref[...])
    lr = lbre_ref[...]
    li = lbim_ref[...]
    hr = s0re_ref[...]
    hi = s0im_ref[...]
    for t in range(ns):
        sl = slice(t * nb, (t + 1) * nb)
        nr = lr * hr - li * hi + xs_buf[sl, 0:SSM_FLAT]
        ni = lr * hi + li * hr + xs_buf[sl, SSM_FLAT:2 * SSM_FLAT]
        hr, hi = nr, ni
        xs_buf[sl, 0:SSM_FLAT] = hr
        xs_buf[sl, SSM_FLAT:2 * SSM_FLAT] = hi
    sre_ref[...] = hr
    sim_ref[...] = hi
    y = _dot(xs_buf[...].astype(BF16), cbig_ref[...])
    ys = _s5_post(y, u, d_ref[...], wglu_ref[...], gout[:, :SSM_WIDTH])

    npre = (CONV_WIDTH - 1) * nb
    xp_buf[0:npre, :] = conv0_ref[...]
    xp_buf[npre:npre + rows, :] = xl
    cw = convw_ref[...]
    xc = convb_ref[...] + cw[0:1, :] * xp_buf[0:rows, :]
    for t in range(1, CONV_WIDTH):
        xc = xc + cw[t:t + 1, :] * xp_buf[t * nb:t * nb + rows, :]
    convout_ref[...] = xp_buf[rows:rows + npre, :]
    a, b = _lru_gates(xc, wax_ref[...], ba_ref[...], bx_ref[...], lam_ref[...])
    a_buf[...] = a
    b_buf[...] = b
    hh = lru0_ref[...]
    for t in range(ns):
        sl = slice(t * nb, (t + 1) * nb)
        hh = a_buf[sl, :] * hh + b_buf[sl, :]
        hl_buf[sl, :] = hh
    lruout_ref[...] = hh
    yl = jax.nn.gelu(gl) * hl_buf[...]

    _put(q_buf, proj[:, OFF_Q:OFF_K])

    def attn_stream(bi, _):
        qb = _get_strided(q_buf, bi, ns, nb).astype(BF16)
        kn = kout_ref[pl.ds(bi, ns, stride=nb), :]
        vn = vout_ref[pl.ds(bi, ns, stride=nb), :]
        kk = jnp.concatenate([ck_ref[bi], kn], axis=0).astype(BF16)
        vv = jnp.concatenate([cv_ref[bi], vn], axis=0).astype(BF16)
        _put_strided(ya_buf, bi, ns, nb, _attend(qb, kk, vv, sink_ref, ns, None))
        return 0

    lax.fori_loop(0, nb, attn_stream, 0)
    o_ref[...] = _mix_out(x, ys, _get(ya_buf), yl, gout, wout_ref[...])


def _mixer_sample(x, ck, cv, s0re, s0im, conv0, lru0, p, nb, ns):
    rows = nb * ns
    vm = pl.BlockSpec(memory_space=pltpu.VMEM)
    in_specs = [vm] * 15 + [pl.BlockSpec(memory_space=pltpu.SMEM)] + [vm] * 8
    out_shape = [
        jax.ShapeDtypeStruct((rows, D_MODEL), F32),
        jax.ShapeDtypeStruct((rows, KV_WIDTH), F32), jax.ShapeDtypeStruct((rows, KV_WIDTH), F32),
        jax.ShapeDtypeStruct((nb, SSM_FLAT), F32), jax.ShapeDtypeStruct((nb, SSM_FLAT), F32),
        jax.ShapeDtypeStruct(((CONV_WIDTH - 1) * nb, LRU_WIDTH), F32), jax.ShapeDtypeStruct((nb, LRU_WIDTH), F32),
    ]
    scratch = [
        pltpu.VMEM((rows, 2 * SSM_FLAT), F32),
        pltpu.VMEM((rows + (CONV_WIDTH - 1) * nb, LRU_WIDTH), F32),
        pltpu.VMEM((rows, LRU_WIDTH), F32), pltpu.VMEM((rows, LRU_WIDTH), F32), pltpu.VMEM((rows, LRU_WIDTH), F32),
        pltpu.VMEM((ATT_WIDTH // LANES, rows, LANES), F32),
        pltpu.VMEM((ATT_WIDTH // LANES, rows, LANES), F32),
    ]
    return pl.pallas_call(
        functools.partial(_mixer_sample_kernel, nb=nb, ns=ns),
        in_specs=in_specs, out_specs=[vm] * 7, out_shape=out_shape, scratch_shapes=scratch,
        compiler_params=pltpu.CompilerParams(vmem_limit_bytes=VMEM_LIMIT),
        name="mixer_sample",
    )(x, ck, cv, s0re, s0im, conv0, lru0,
      p["gmix"], p["win"], p["lb_re"], p["lb_im"], p["bbig"], p["cbig"], p["d"], p["wglu"], p["sink"],
      p["convw"], p["convb"], p["wax"], p["b_a"], p["b_x"], p["lam"], p["gout"], p["wout"])


def _block_diag(w):
    n, i, o = w.shape
    eye = jnp.eye(n, dtype=w.dtype)
    return jnp.einsum("nm,nio->nimo", eye, w).reshape(n * i, n * o)


def _layer_params(l, mix_norm, w_in, ssm_a_re, ssm_a_im, ssm_log_dt, ssm_b_re, ssm_b_im, ssm_c_re, ssm_c_im,
                  ssm_d, ssm_w_glu, attn_sink, conv_w, conv_b, lru_w_a, lru_b_a, lru_w_x, lru_b_x, lru_lambda,
                  out_norm, w_out):
    a_re = ssm_a_re[l]
    a_im = ssm_a_im[l]
    dt = jnp.exp(ssm_log_dt[l])[:, None]
    mag = jnp.exp(a_re * dt)
    lb_re = mag * jnp.cos(a_im * dt)
    lb_im = mag * jnp.sin(a_im * dt)
    den = a_re * a_re + a_im * a_im
    nr = lb_re - 1.0
    k_re = (nr * a_re + lb_im * a_im) / den
    k_im = (lb_im * a_re - nr * a_im) / den
    b_re = ssm_b_re[l]
    b_im = ssm_b_im[l]
    bb_re = k_re[..., None] * b_re - k_im[..., None] * b_im
    bb_im = k_re[..., None] * b_im + k_im[..., None] * b_re
    to_gh_p = lambda m: jnp.transpose(m, (0, 2, 1))
    bbig = jnp.concatenate([_block_diag(to_gh_p(bb_re)), _block_diag(to_gh_p(bb_im))], axis=1)
    to_gp_h = lambda m: jnp.transpose(m, (0, 2, 1))
    cbig = jnp.concatenate([_block_diag(to_gp_h(ssm_c_re[l])), -_block_diag(to_gp_h(ssm_c_im[l]))], axis=0)
    row = lambda vec: vec.reshape(1, -1)
    return {
        "gmix": row(mix_norm[l]), "win": w_in[l].astype(BF16),
        "lb_re": row(lb_re), "lb_im": row(lb_im),
        "bbig": bbig.astype(BF16), "cbig": cbig.astype(BF16),
        "d": row(ssm_d[l]), "wglu": ssm_w_glu[l].astype(BF16), "sink": attn_sink[l],
        "convw": conv_w[l], "convb": row(conv_b[l]),
        "wax": jnp.concatenate([_block_diag(lru_w_a[l]), _block_diag(lru_w_x[l])], axis=1).astype(BF16),
        "b_a": row(lru_b_a[l]), "b_x": row(lru_b_x[l]), "lam": row(lru_lambda[l]),
        "gout": row(out_norm[l]), "wout": w_out[l].astype(BF16),
    }


def kernel(x_prompt, x_sample, cache_k, cache_v, state_ssm_re, state_ssm_im, state_conv, state_lru,
           ffn1_norm, ffn1_w_gate, ffn1_w_up, ffn1_w_down, mix_norm, w_in,
           ssm_a_re, ssm_a_im, ssm_log_dt, ssm_b_re, ssm_b_im, ssm_c_re, ssm_c_im, ssm_d, ssm_w_glu,
           attn_sink, conv_w, conv_b, lru_w_a, lru_b_a, lru_w_x, lru_b_x, lru_lambda,
           out_norm, w_out, ffn2_norm, ffn2_w_gate, ffn2_w_up, ffn2_w_down, final_norm):
    bn, L, _ = x_prompt.shape
    nb, ns, _ = x_sample.shape
    row = lambda vec: vec.reshape(1, -1)
    gfin = row(final_norm)

    xp = x_prompt.reshape(bn * L, D_MODEL)
    xs = jnp.transpose(x_sample, (1, 0, 2)).reshape(ns * nb, D_MODEL)
    prompt_states = [[] for _ in range(6)]
    sample_states = [[] for _ in range(6)]
    for l in range(DEPTH):
        p = _layer_params(l, mix_norm, w_in, ssm_a_re, ssm_a_im, ssm_log_dt, ssm_b_re, ssm_b_im, ssm_c_re,
                          ssm_c_im, ssm_d, ssm_w_glu, attn_sink, conv_w, conv_b, lru_w_a, lru_b_a, lru_w_x,
                          lru_b_x, lru_lambda, out_norm, w_out)
        f1 = (row(ffn1_norm[l]), ffn1_w_gate[l].astype(BF16), ffn1_w_up[l].astype(BF16),
              ffn1_w_down[l].astype(BF16))
        f2 = (row(ffn2_norm[l]), ffn2_w_gate[l].astype(BF16), ffn2_w_up[l].astype(BF16),
              ffn2_w_down[l].astype(BF16))
        is_last = l == DEPTH - 1

        xp = _ffn(xp, *f1, gfin, False)
        xp3, kp, vp, srp, sip, cvp, lrp = _mixer_prompt(xp.reshape(bn, L, D_MODEL), p)
        xp = _ffn(xp3.reshape(bn * L, D_MODEL), *f2, gfin, is_last)
        for lst, s in zip(prompt_states, (
                kp.reshape(bn, WINDOW, N_KV, HEAD_DIM), vp.reshape(bn, WINDOW, N_KV, HEAD_DIM),
                srp.reshape(bn, SSM_GROUPS, SSM_STATE), sip.reshape(bn, SSM_GROUPS, SSM_STATE),
                cvp, lrp.reshape(bn, LRU_WIDTH))):
            lst.append(s)

        xs = _ffn(xs, *f1, gfin, False)
        ck = cache_k[l].reshape(nb, -1, KV_WIDTH)
        cv = cache_v[l].reshape(nb, -1, KV_WIDTH)
        conv0 = jnp.transpose(state_conv[l], (1, 0, 2)).reshape((CONV_WIDTH - 1) * nb, LRU_WIDTH)
        xs, ks, vs, srs, sis, cvs, lrs = _mixer_sample(
            xs, ck, cv, state_ssm_re[l].reshape(nb, SSM_FLAT), state_ssm_im[l].reshape(nb, SSM_FLAT),
            conv0, state_lru[l], p, nb, ns)
        xs = _ffn(xs, *f2, gfin, is_last)
        unflip = lambda m, w: jnp.transpose(m.reshape(-1, nb, w), (1, 0, 2))
        for lst, s in zip(sample_states, (
                unflip(ks, KV_WIDTH).reshape(nb, ns, N_KV, HEAD_DIM),
                unflip(vs, KV_WIDTH).reshape(nb, ns, N_KV, HEAD_DIM),
                srs.reshape(nb, SSM_GROUPS, SSM_STATE), sis.reshape(nb, SSM_GROUPS, SSM_STATE),
                unflip(cvs, LRU_WIDTH), lrs)):
            lst.append(s)

    y_prompt = xp.reshape(bn, L, D_MODEL)
    y_sample = jnp.transpose(xs.reshape(ns, nb, D_MODEL), (1, 0, 2))
    return (y_prompt, y_sample, *[jnp.stack(c, axis=0) for c in prompt_states],
            *[jnp.stack(c, axis=0) for c in sample_states])
```

```python
import functools
import math

import jax
import jax.numpy as jnp
from jax import lax
from jax.experimental import pallas as pl
from jax.experimental.pallas import tpu as pltpu

F32 = jnp.float32
BF16 = jnp.bfloat16

D_MODEL = 1024
D_FF = 2816
DEPTH = 2
CHUNK = 64
SSM_WIDTH = 256
SSM_GROUP = 16
SSM_GROUPS = 16
SSM_STATE = 64
SSM_FLAT = SSM_GROUPS * SSM_STATE
HEAD_DIM = 64
ATT_WIDTH = 512
N_HEADS = 8
N_KV = 2
Q_PER_KV = 4
KV_WIDTH = 128
WINDOW = 128
LRU_WIDTH = 256
LRU_BLOCKS = 4
LRU_BLOCK = 64
CONV_WIDTH = 4
LRU_C = 8.0
OFF_Q = SSM_WIDTH
OFF_K = OFF_Q + ATT_WIDTH
OFF_V = OFF_K + KV_WIDTH
OFF_LX = OFF_V + KV_WIDTH
OFF_LG = OFF_LX + LRU_WIDTH
IN_WIDTH = OFF_LG + LRU_WIDTH
ATT_SCALE = HEAD_DIM ** -0.5
EPS = 1e-6
NEG_INF = -1e30

SUBLANES = 8
LANES = 128
FFN_ROWS = 512
FFN_COLS = 256
STAGE_SLOTS = 3
STAGE_IN_ROWS = 128
STAGE_OUT_ROWS = 256
MIX_ROWS = 512
SEGS = SUBLANES
SEG_LEN = MIX_ROWS // SEGS
VMEM_LIMIT = 56 * 1024 * 1024


def _rms(x, g):
    return x * lax.rsqrt(jnp.mean(x * x, axis=-1, keepdims=True) + EPS) * g


def _dot(a, b):
    return jnp.dot(a, b, preferred_element_type=F32)


def _cmul(ar, ai, br, bi):
    return ar * br - ai * bi, ar * bi + ai * br


def _put(ref, val):
    for c in range(ref.shape[0]):
        ref[c] = val[:, c * LANES:(c + 1) * LANES]


def _get(ref):
    return jnp.concatenate([ref[c] for c in range(ref.shape[0])], axis=1)


def _get_strided(ref, start, n, stride):
    return jnp.concatenate(
        [ref[c, pl.ds(start, n, stride=stride), :] for c in range(ref.shape[0])], axis=1)


def _put_strided(ref, start, n, stride, val):
    for c in range(ref.shape[0]):
        ref[c, pl.ds(start, n, stride=stride), :] = val[:, c * LANES:(c + 1) * LANES]


def _weight_jobs(wg_hbm, wu_hbm, wd_hbm, wg_v, wu_v, wd_v, stage_in, stage_out, sem):
    jobs = []
    used = [0, 0]

    def add(src, dst, stage, kind, r0, n):
        slot = used[kind] % STAGE_SLOTS
        used[kind] += 1
        copy = pltpu.make_async_copy(src.at[pl.ds(r0, n), :], stage.at[slot], sem.at[kind, slot])
        jobs.append((copy, stage.at[slot], dst, r0, n))

    for r0 in range(0, D_MODEL, STAGE_IN_ROWS):
        add(wg_hbm, wg_v, stage_in, 0, r0, STAGE_IN_ROWS)
        add(wu_hbm, wu_v, stage_in, 0, r0, STAGE_IN_ROWS)
    for r0 in range(0, D_FF, STAGE_OUT_ROWS):
        add(wd_hbm, wd_v, stage_out, 1, r0, STAGE_OUT_ROWS)
    return jobs


def _ffn_kernel(xp_ref, xs_ref, g_ref, wg_hbm, wu_hbm, wd_hbm, gf_ref, op_ref, os_ref,
                wg_ref, wu_ref, wd_ref, stage_in, stage_out, sem, *, final_norm, prompt_tiles):
    i = pl.program_id(0)

    @pl.when(i == 0)
    def _():
        jobs = _weight_jobs(wg_hbm, wu_hbm, wd_hbm, wg_ref, wu_ref, wd_ref, stage_in, stage_out, sem)
        ahead = STAGE_SLOTS - 1
        for k in range(min(ahead, len(jobs))):
            jobs[k][0].start()
        for k, (copy, staged, dst, r0, n) in enumerate(jobs):
            if k + ahead < len(jobs):
                jobs[k + ahead][0].start()
            copy.wait()
            dst[r0:r0 + n, :] = staged[...].astype(BF16)

    x = jnp.where(i < prompt_tiles, xp_ref[...], xs_ref[...])
    h = _rms(x, g_ref[...]).astype(BF16)
    acc = jnp.zeros(x.shape, F32)
    for c in range(D_FF // FFN_COLS):
        sl = slice(c * FFN_COLS, (c + 1) * FFN_COLS)
        g = _dot(h, wg_ref[:, sl])
        u = _dot(h, wu_ref[:, sl])
        a = (g * jax.nn.sigmoid(g) * u).astype(BF16)
        acc = acc + _dot(a, wd_ref[sl, :])
    y = x + 0.5 * acc
    if final_norm:
        y = _rms(y, gf_ref[...])

    @pl.when(i < prompt_tiles)
    def _():
        op_ref[...] = y

    @pl.when(i >= prompt_tiles)
    def _():
        os_ref[...] = y


def _ffn(xp, xs, g, wg, wu, wd, gf, final_norm):
    prompt_tiles = xp.shape[0] // FFN_ROWS
    assert xs.shape[0] == FFN_ROWS
    const = lambda i: (0, 0)
    prompt_map = lambda i: (jnp.minimum(i, prompt_tiles - 1), 0)
    hbm = pl.BlockSpec(memory_space=pl.ANY)
    return pl.pallas_call(
        functools.partial(_ffn_kernel, final_norm=final_norm, prompt_tiles=prompt_tiles),
        grid=(prompt_tiles + 1,),
        in_specs=[
            pl.BlockSpec((FFN_ROWS, D_MODEL), prompt_map),
            pl.BlockSpec((FFN_ROWS, D_MODEL), const),
            pl.BlockSpec((1, D_MODEL), const),
            hbm, hbm, hbm,
            pl.BlockSpec((1, D_MODEL), const),
        ],
        out_specs=[pl.BlockSpec((FFN_ROWS, D_MODEL), prompt_map), pl.BlockSpec((FFN_ROWS, D_MODEL), const)],
        out_shape=[jax.ShapeDtypeStruct(xp.shape, F32), jax.ShapeDtypeStruct(xs.shape, F32)],
        scratch_shapes=[
            pltpu.VMEM((D_MODEL, D_FF), BF16), pltpu.VMEM((D_MODEL, D_FF), BF16), pltpu.VMEM((D_FF, D_MODEL), BF16),
            pltpu.VMEM((STAGE_SLOTS, STAGE_IN_ROWS, D_FF), F32),
            pltpu.VMEM((STAGE_SLOTS, STAGE_OUT_ROWS, D_MODEL), F32),
            pltpu.SemaphoreType.DMA((2, STAGE_SLOTS)),
        ],
        compiler_params=pltpu.CompilerParams(
            dimension_semantics=("arbitrary",), vmem_limit_bytes=VMEM_LIMIT),
        name="ffn",
    )(xp, xs, g, wg, wu, wd, gf)


def _log_sigmoid(x):
    return jnp.minimum(x, 0.0) - jnp.log1p(jnp.exp(-jnp.abs(x)))


def _s5_post(y, u, d, wglu, g):
    z = jax.nn.gelu(y + d * u)
    out = z * jax.nn.sigmoid(_dot(z.astype(BF16), wglu))
    return _rms(out, g)


def _lru_gates(xc, wax, b_a, b_x, lam):
    ga = _dot(xc.astype(BF16), wax)
    r = jax.nn.sigmoid(ga[:, :LRU_WIDTH] + b_a)
    i = jax.nn.sigmoid(ga[:, LRU_WIDTH:] + b_x)
    log_a = LRU_C * r * _log_sigmoid(lam)
    a = jnp.exp(log_a)
    t = jnp.tanh(log_a)
    mult = jnp.sqrt(-2.0 * t / (1.0 - t))
    return a, mult * i * xc


def _sink_softmax(s, sink_col):
    m = jnp.maximum(jnp.max(s, axis=-1, keepdims=True), sink_col)
    e = jnp.exp(s - m)
    den = jnp.sum(e, axis=-1, keepdims=True) + jnp.exp(sink_col - m)
    return e / den


def _attend(q_rows, kk, vv, sink_ref, nq, valid):
    outs = []
    for kvh in range(N_KV):
        qs = jnp.concatenate(
            [q_rows[:, (kvh * Q_PER_KV + g) * HEAD_DIM:(kvh * Q_PER_KV + g + 1) * HEAD_DIM]
             for g in range(Q_PER_KV)], axis=0)
        kh = kk[:, kvh * HEAD_DIM:(kvh + 1) * HEAD_DIM]
        vh = vv[:, kvh * HEAD_DIM:(kvh + 1) * HEAD_DIM]
        s = lax.dot_general(qs, kh, (((1,), (1,)), ((), ())), preferred_element_type=F32) * ATT_SCALE
        if valid is not None:
            s = jnp.where(valid, s, NEG_INF)
        sink_col = jnp.concatenate(
            [jnp.full((nq, 1), sink_ref[kvh * Q_PER_KV + g], F32) for g in range(Q_PER_KV)], axis=0)
        p = _sink_softmax(s, sink_col).astype(BF16)
        o = _dot(p, vh)
        outs.extend(o[g * nq:(g + 1) * nq, :] for g in range(Q_PER_KV))
    return jnp.concatenate(outs, axis=1)


def _mix_out(x, ys, ya, yl, gout, wout):
    ycat = jnp.concatenate([
        ys,
        _rms(ya, gout[:, SSM_WIDTH:SSM_WIDTH + ATT_WIDTH]),
        _rms(yl, gout[:, SSM_WIDTH + ATT_WIDTH:]),
    ], axis=1).astype(BF16)
    return x + _dot(ycat, wout)


def _mixer_prompt_kernel(
        x_ref, gmix_ref, win_ref, lbre_ref, lbim_ref, bbig_ref, cbig_ref, d_ref, wglu_ref, sink_ref,
        convw_ref, convb_ref, wax_ref, ba_ref, bx_ref, lam_ref, gout_ref, wout_ref,
        o_ref, kout_ref, vout_ref, sre_ref, sim_ref, convout_ref, lruout_ref,
        st_re, st_im, st_lru, u_buf, up_buf, xs_buf, yp_buf, ys_buf, xl_buf, a_buf, b_buf, hl_buf, lb_buf,
        qz_buf, k_buf, vt_buf, yt_buf):
    j = pl.program_id(1)
    last = pl.num_programs(1) - 1
    T = MIX_ROWS

    @pl.when(j == 0)
    def _():
        st_re[...] = jnp.zeros_like(st_re)
        st_im[...] = jnp.zeros_like(st_im)
        st_lru[...] = jnp.zeros_like(st_lru)
        xl_buf[0:SUBLANES, :] = jnp.zeros((SUBLANES, LRU_WIDTH), F32)
        k_buf[0:WINDOW, :] = jnp.zeros((WINDOW, KV_WIDTH), BF16)
        vt_buf[0] = jnp.zeros((KV_WIDTH, WINDOW), BF16)

    x = x_ref[0]
    h = _rms(x, gmix_ref[...]).astype(BF16)
    proj = _dot(h, win_ref[...])
    u = proj[:, :OFF_Q]
    k = proj[:, OFF_K:OFF_V]
    v = proj[:, OFF_V:OFF_LX]
    xl = proj[:, OFF_LX:OFF_LG]
    gl = proj[:, OFF_LG:]
    gout = gout_ref[...]

    _put(u_buf, u)
    for i in range(SEG_LEN):
        up_buf[i * SEGS:(i + 1) * SEGS, :] = _get_strided(u_buf, i, SEGS, SEG_LEN)
    up = up_buf[...]
    xs_buf[...] = _dot(up.astype(BF16), bbig_ref[...])
    lr = lbre_ref[...]
    li = lbim_ref[...]
    lb_buf[0] = jnp.broadcast_to(lr, (SEGS, SSM_FLAT))
    lb_buf[1] = jnp.broadcast_to(li, (SEGS, SSM_FLAT))

    def s5_step(i, carry, store):
        hr, hi = carry
        r0 = pl.multiple_of(i * SEGS, SEGS)
        xr = xs_buf[pl.ds(r0, SEGS), 0:SSM_FLAT]
        xi = xs_buf[pl.ds(r0, SEGS), SSM_FLAT:2 * SSM_FLAT]
        lrb = lb_buf[0]
        lib = lb_buf[1]
        nr = lrb * hr - lib * hi + xr
        ni = lrb * hi + lib * hr + xi
        if store:
            xs_buf[pl.ds(r0, SEGS), 0:SSM_FLAT] = nr
            xs_buf[pl.ds(r0, SEGS), SSM_FLAT:2 * SSM_FLAT] = ni
        return nr, ni

    zero_seg = jnp.zeros((SEGS, SSM_FLAT), F32)
    loc_r, loc_i = lax.fori_loop(0, SEG_LEN, functools.partial(s5_step, store=False), (zero_seg, zero_seg))
    pr, pi = lr, li
    for _ in range(int(math.log2(SEG_LEN))):
        pr, pi = _cmul(pr, pi, pr, pi)
    gr, gi = st_re[...], st_im[...]
    rows_r, rows_i = [], []
    for s in range(SEGS):
        rows_r.append(gr)
        rows_i.append(gi)
        ar, ai = _cmul(pr, pi, gr, gi)
        gr = ar + loc_r[s:s + 1, :]
        gi = ai + loc_i[s:s + 1, :]
    st_re[...] = gr
    st_im[...] = gi
    lax.fori_loop(0, SEG_LEN, functools.partial(s5_step, store=True),
                  (jnp.concatenate(rows_r, axis=0), jnp.concatenate(rows_i, axis=0)))
    y = _dot(xs_buf[...].astype(BF16), cbig_ref[...])
    yp_buf[...] = _s5_post(y, up, d_ref[...], wglu_ref[...], gout[:, :SSM_WIDTH])
    for i in range(SEG_LEN):
        _put_strided(ys_buf, i, SEGS, SEG_LEN, yp_buf[i * SEGS:(i + 1) * SEGS, :])

    xl_buf[SUBLANES:SUBLANES + T, :] = xl
    cw = convw_ref[...]
    xc = convb_ref[...] + cw[0:1, :] * xl_buf[SUBLANES - 3:SUBLANES - 3 + T, :]
    for t in range(1, CONV_WIDTH):
        xc = xc + cw[t:t + 1, :] * xl_buf[SUBLANES - 3 + t:SUBLANES - 3 + t + T, :]
    tail = xl_buf[T:T + SUBLANES, :]
    xl_buf[0:SUBLANES, :] = tail
    a, b = _lru_gates(xc, wax_ref[...], ba_ref[...], bx_ref[...], lam_ref[...])
    a3 = a.reshape(T // SUBLANES, SUBLANES, LRU_WIDTH)
    b3 = b.reshape(T // SUBLANES, SUBLANES, LRU_WIDTH)
    slab_row = lax.broadcasted_iota(jnp.int32, a3.shape, 1)
    shift = 1
    while shift < SUBLANES:
        keep = slab_row >= shift
        b3 = jnp.where(keep, a3 * pltpu.roll(b3, shift, axis=1) + b3, b3)
        a3 = jnp.where(keep, a3 * pltpu.roll(a3, shift, axis=1), a3)
        shift *= 2
    a_buf[...] = a3.reshape(T, LRU_WIDTH)
    b_buf[...] = b3.reshape(T, LRU_WIDTH)

    def lru_slab(i, carry):
        r0 = pl.multiple_of(i * SUBLANES, SUBLANES)
        hh = a_buf[pl.ds(r0, SUBLANES), :] * carry + b_buf[pl.ds(r0, SUBLANES), :]
        hl_buf[pl.ds(r0, SUBLANES), :] = hh
        return jnp.broadcast_to(hh[SUBLANES - 1:SUBLANES, :], (SUBLANES, LRU_WIDTH))

    lru_carry = lax.fori_loop(0, T // SUBLANES, lru_slab,
                              jnp.broadcast_to(st_lru[...], (SUBLANES, LRU_WIDTH)), unroll=4)
    gl_state = lru_carry[0:1, :]
    st_lru[...] = gl_state
    yl = jax.nn.gelu(gl) * hl_buf[...]

    lo = lax.broadcasted_iota(jnp.int32, (T, LANES), 1) < HEAD_DIM
    for jp in range(N_HEADS // 2):
        pair = proj[:, OFF_Q + jp * LANES:OFF_Q + (jp + 1) * LANES]
        swapped = pltpu.roll(pair, HEAD_DIM, axis=1)
        if jp < N_HEADS // 4:
            even, odd = jnp.where(lo, pair, 0.0), jnp.where(lo, swapped, 0.0)
        else:
            even, odd = jnp.where(lo, 0.0, swapped), jnp.where(lo, 0.0, pair)
        qz_buf[2 * jp] = even.astype(BF16)
        qz_buf[2 * jp + 1] = odd.astype(BF16)
    k_buf[WINDOW:WINDOW + T, :] = k.astype(BF16)
    vt = v.T.astype(BF16)
    for i in range(T // WINDOW):
        vt_buf[i + 1] = vt[:, i * WINDOW:(i + 1) * WINDOW]

    span = 2 * CHUNK
    krow = lax.broadcasted_iota(jnp.int32, (2 * span, 2 * span), 0)
    qcol = lax.broadcasted_iota(jnp.int32, (2 * span, 2 * span), 1) % span
    band = jnp.logical_or(jnp.logical_and(qcol < CHUNK, krow < WINDOW + CHUNK),
                          jnp.logical_and(qcol >= CHUNK, krow >= CHUNK))
    for m in range(T // span):
        valid = band if m > 0 else jnp.logical_and(band, jnp.logical_or(krow >= WINDOW, j > 0))
        kwin = k_buf[m * span:(m + 2) * span, :]
        vtwin = jnp.concatenate([vt_buf[m], vt_buf[m + 1]], axis=1)
        for hp in range(N_HEADS // 2):
            kvh = hp // (Q_PER_KV // 2)
            qz = jnp.concatenate([qz_buf[2 * hp, m * span:(m + 1) * span, :],
                                  qz_buf[2 * hp + 1, m * span:(m + 1) * span, :]], axis=0)
            s = lax.dot_general(kwin, qz, (((1,), (1,)), ((), ())), preferred_element_type=F32) * ATT_SCALE
            s = jnp.where(valid, s, NEG_INF)
            sink_row = jnp.concatenate([jnp.full((1, span), sink_ref[2 * hp], F32),
                                        jnp.full((1, span), sink_ref[2 * hp + 1], F32)], axis=1)
            mx = jnp.maximum(jnp.max(s, axis=0, keepdims=True), sink_row)
            e = jnp.exp(s - mx)
            den = jnp.sum(e, axis=0, keepdims=True) + jnp.exp(sink_row - mx)
            o = _dot(vtwin, e.astype(BF16))[kvh * HEAD_DIM:(kvh + 1) * HEAD_DIM, :] * (1.0 / den)
            yt_buf[2 * hp * HEAD_DIM:(2 * hp + 1) * HEAD_DIM, m * span:(m + 1) * span] = o[:, :span]
            yt_buf[(2 * hp + 1) * HEAD_DIM:(2 * hp + 2) * HEAD_DIM, m * span:(m + 1) * span] = o[:, span:]
    k_buf[0:WINDOW, :] = k_buf[T:T + WINDOW, :]
    vt_buf[0] = vt_buf[T // WINDOW]

    o_ref[0] = _mix_out(x, _get(ys_buf), yt_buf[...].T, yl, gout, wout_ref[...])

    @pl.when(j == last)
    def _():
        kout_ref[0] = k[T - WINDOW:, :]
        vout_ref[0] = v[T - WINDOW:, :]
        sre_ref[0] = gr
        sim_ref[0] = gi
        convout_ref[0] = tail[SUBLANES - (CONV_WIDTH - 1):, :]
        lruout_ref[0] = gl_state


def _mixer_prompt(x, p):
    bn, L, _ = x.shape
    T = MIX_ROWS
    const2 = lambda b, j: (0, 0)
    vspec = lambda shape: pl.BlockSpec(shape, const2)
    per_b = lambda shape: pl.BlockSpec((1,) + shape, lambda b, j: (b, 0, 0))
    in_specs = [
        pl.BlockSpec((1, T, D_MODEL), lambda b, j: (b, j, 0)),
        vspec((1, D_MODEL)), vspec((D_MODEL, IN_WIDTH)),
        vspec((1, SSM_FLAT)), vspec((1, SSM_FLAT)),
        vspec((SSM_WIDTH, 2 * SSM_FLAT)), vspec((2 * SSM_FLAT, SSM_WIDTH)),
        vspec((1, SSM_WIDTH)), vspec((SSM_WIDTH, SSM_WIDTH)),
        pl.BlockSpec(memory_space=pltpu.SMEM),
        vspec((CONV_WIDTH, LRU_WIDTH)), vspec((1, LRU_WIDTH)),
        vspec((LRU_WIDTH, 2 * LRU_WIDTH)), vspec((1, LRU_WIDTH)), vspec((1, LRU_WIDTH)), vspec((1, LRU_WIDTH)),
        vspec((1, D_MODEL)), vspec((D_MODEL, D_MODEL)),
    ]
    out_specs = [
        pl.BlockSpec((1, T, D_MODEL), lambda b, j: (b, j, 0)),
        per_b((WINDOW, KV_WIDTH)), per_b((WINDOW, KV_WIDTH)),
        per_b((1, SSM_FLAT)), per_b((1, SSM_FLAT)),
        per_b((CONV_WIDTH - 1, LRU_WIDTH)), per_b((1, LRU_WIDTH)),
    ]
    out_shape = [
        jax.ShapeDtypeStruct((bn, L, D_MODEL), F32),
        jax.ShapeDtypeStruct((bn, WINDOW, KV_WIDTH), F32), jax.ShapeDtypeStruct((bn, WINDOW, KV_WIDTH), F32),
        jax.ShapeDtypeStruct((bn, 1, SSM_FLAT), F32), jax.ShapeDtypeStruct((bn, 1, SSM_FLAT), F32),
        jax.ShapeDtypeStruct((bn, CONV_WIDTH - 1, LRU_WIDTH), F32), jax.ShapeDtypeStruct((bn, 1, LRU_WIDTH), F32),
    ]
    planes = lambda width: pltpu.VMEM((width // LANES, T, LANES), F32)
    scratch = [
        pltpu.VMEM((1, SSM_FLAT), F32), pltpu.VMEM((1, SSM_FLAT), F32), pltpu.VMEM((1, LRU_WIDTH), F32),
        planes(SSM_WIDTH), pltpu.VMEM((T, SSM_WIDTH), F32),
        pltpu.VMEM((T, 2 * SSM_FLAT), F32),
        pltpu.VMEM((T, SSM_WIDTH), F32), planes(SSM_WIDTH),
        pltpu.VMEM((T + SUBLANES, LRU_WIDTH), F32),
        pltpu.VMEM((T, LRU_WIDTH), F32), pltpu.VMEM((T, LRU_WIDTH), F32), pltpu.VMEM((T, LRU_WIDTH), F32),
        pltpu.VMEM((2, SEGS, SSM_FLAT), F32),
        pltpu.VMEM((N_HEADS, T, LANES), BF16),
        pltpu.VMEM((T + WINDOW, KV_WIDTH), BF16),
        pltpu.VMEM((T // WINDOW + 1, KV_WIDTH, WINDOW), BF16),
        pltpu.VMEM((ATT_WIDTH, T), F32),
    ]
    return pl.pallas_call(
        _mixer_prompt_kernel,
        grid=(bn, L // T),
        in_specs=in_specs, out_specs=out_specs, out_shape=out_shape, scratch_shapes=scratch,
        compiler_params=pltpu.CompilerParams(
            dimension_semantics=("arbitrary", "arbitrary"), vmem_limit_bytes=VMEM_LIMIT),
        name="mixer_prompt",
    )(x, p["gmix"], p["win"], p["lb_re"], p["lb_im"], p["bbig"], p["cbig"], p["d"], p["wglu"], p["sink"],
      p["convw"], p["convb"], p["wax"], p["b_a"], p["b_x"], p["lam"], p["gout"], p["wout"])


def _mixer_sample_kernel(
        x_ref, ck_ref, cv_ref, s0re_ref, s0im_ref, conv0_ref, lru0_ref,
        gmix_ref, win_ref, lbre_ref, lbim_ref, bbig_ref, cbig_ref, d_ref, wglu_ref, sink_ref,
        convw_ref, convb_ref, wax_ref, ba_ref, bx_ref, lam_ref, gout_ref, wout_ref,
        o_ref, kout_ref, vout_ref, sre_ref, sim_ref, convout_ref, lruout_ref,
        xs_buf, xp_buf, a_buf, b_buf, hl_buf, q_buf, ya_buf, *, nb, ns):
    rows = nb * ns
    x = x_ref[...]
    h = _rms(x, gmix_ref[...]).astype(BF16)
    proj = _dot(h, win_ref[...])
    u = proj[:, :OFF_Q]
    k = proj[:, OFF_K:OFF_V]
    v = proj[:, OFF_V:OFF_LX]
    xl = proj[:, OFF_LX:OFF_LG]
    gl = proj[:, OFF_LG:]
    gout = gout_ref[...]
    kout_ref[...] = k
    vout_ref[...] = v

    xs_buf[...] = _dot(u.astype(BF16), bbig_ref[...])
    lr = lbre_ref[...]
    li = lbim_ref[...]
    hr = s0re_ref[...]
    hi = s0im_ref[...]
    for t in range(ns):
        sl = slice(t * nb, (t + 1) * nb)
        nr = lr * hr - li * hi + xs_buf[sl, 0:SSM_FLAT]
        ni = lr * hi + li * hr + xs_buf[sl, SSM_FLAT:2 * SSM_FLAT]
        hr, hi = nr, ni
        xs_buf[sl, 0:SSM_FLAT] = hr
        xs_buf[sl, SSM_FLAT:2 * SSM_FLAT] = hi
    sre_ref[...] = hr
    sim_ref[...] = hi
    y = _dot(xs_buf[...].astype(BF16), cbig_ref[...])
    ys = _s5_post(y, u, d_ref[...], wglu_ref[...], gout[:, :SSM_WIDTH])

    npre = (CONV_WIDTH - 1) * nb
    xp_buf[0:npre, :] = conv0_ref[...]
    xp_buf[npre:npre + rows, :] = xl
    cw = convw_ref[...]
    xc = convb_ref[...] + cw[0:1, :] * xp_buf[0:rows, :]
    for t in range(1, CONV_WIDTH):
        xc = xc + cw[t:t + 1, :] * xp_buf[t * nb:t * nb + rows, :]
    convout_ref[...] = xp_buf[rows:rows + npre, :]
    a, b = _lru_gates(xc, wax_ref[...], ba_ref[...], bx_ref[...], lam_ref[...])
    a_buf[...] = a
    b_buf[...] = b
    hh = lru0_ref[...]
    for t in range(ns):
        sl = slice(t * nb, (t + 1) * nb)
        hh = a_buf[sl, :] * hh + b_buf[sl, :]
        hl_buf[sl, :] = hh
    lruout_ref[...] = hh
    yl = jax.nn.gelu(gl) * hl_buf[...]

    _put(q_buf, proj[:, OFF_Q:OFF_K])

    def attn_stream(bi, _):
        qb = _get_strided(q_buf, bi, ns, nb).astype(BF16)
        kn = kout_ref[pl.ds(bi, ns, stride=nb), :]
        vn = vout_ref[pl.ds(bi, ns, stride=nb), :]
        kk = jnp.concatenate([ck_ref[bi], kn], axis=0).astype(BF16)
        vv = jnp.concatenate([cv_ref[bi], vn], axis=0).astype(BF16)
        _put_strided(ya_buf, bi, ns, nb, _attend(qb, kk, vv, sink_ref, ns, None))
        return 0

    lax.fori_loop(0, nb, attn_stream, 0)
    o_ref[...] = _mix_out(x, ys, _get(ya_buf), yl, gout, wout_ref[...])


def _mixer_sample(x, ck, cv, s0re, s0im, conv0, lru0, p, nb, ns):
    rows = nb * ns
    vm = pl.BlockSpec(memory_space=pltpu.VMEM)
    in_specs = [vm] * 15 + [pl.BlockSpec(memory_space=pltpu.SMEM)] + [vm] * 8
    out_shape = [
        jax.ShapeDtypeStruct((rows, D_MODEL), F32),
        jax.ShapeDtypeStruct((rows, KV_WIDTH), F32), jax.ShapeDtypeStruct((rows, KV_WIDTH), F32),
        jax.ShapeDtypeStruct((nb, SSM_FLAT), F32), jax.ShapeDtypeStruct((nb, SSM_FLAT), F32),
        jax.ShapeDtypeStruct(((CONV_WIDTH - 1) * nb, LRU_WIDTH), F32), jax.ShapeDtypeStruct((nb, LRU_WIDTH), F32),
    ]
    scratch = [
        pltpu.VMEM((rows, 2 * SSM_FLAT), F32),
        pltpu.VMEM((rows + (CONV_WIDTH - 1) * nb, LRU_WIDTH), F32),
        pltpu.VMEM((rows, LRU_WIDTH), F32), pltpu.VMEM((rows, LRU_WIDTH), F32), pltpu.VMEM((rows, LRU_WIDTH), F32),
        pltpu.VMEM((ATT_WIDTH // LANES, rows, LANES), F32),
        pltpu.VMEM((ATT_WIDTH // LANES, rows, LANES), F32),
    ]
    return pl.pallas_call(
        functools.partial(_mixer_sample_kernel, nb=nb, ns=ns),
        in_specs=in_specs, out_specs=[vm] * 7, out_shape=out_shape, scratch_shapes=scratch,
        compiler_params=pltpu.CompilerParams(vmem_limit_bytes=VMEM_LIMIT),
        name="mixer_sample",
    )(x, ck, cv, s0re, s0im, conv0, lru0,
      p["gmix"], p["win"], p["lb_re"], p["lb_im"], p["bbig"], p["cbig"], p["d"], p["wglu"], p["sink"],
      p["convw"], p["convb"], p["wax"], p["b_a"], p["b_x"], p["lam"], p["gout"], p["wout"])


def _block_diag(w):
    n, i, o = w.shape
    eye = jnp.eye(n, dtype=w.dtype)
    return jnp.einsum("nm,nio->nimo", eye, w).reshape(n * i, n * o)


def _layer_params(l, mix_norm, w_in, ssm_a_re, ssm_a_im, ssm_log_dt, ssm_b_re, ssm_b_im, ssm_c_re, ssm_c_im,
                  ssm_d, ssm_w_glu, attn_sink, conv_w, conv_b, lru_w_a, lru_b_a, lru_w_x, lru_b_x, lru_lambda,
                  out_norm, w_out):
    a_re = ssm_a_re[l]
    a_im = ssm_a_im[l]
    dt = jnp.exp(ssm_log_dt[l])[:, None]
    mag = jnp.exp(a_re * dt)
    lb_re = mag * jnp.cos(a_im * dt)
    lb_im = mag * jnp.sin(a_im * dt)
    den = a_re * a_re + a_im * a_im
    nr = lb_re - 1.0
    k_re = (nr * a_re + lb_im * a_im) / den
    k_im = (lb_im * a_re - nr * a_im) / den
    b_re = ssm_b_re[l]
    b_im = ssm_b_im[l]
    bb_re = k_re[..., None] * b_re - k_im[..., None] * b_im
    bb_im = k_re[..., None] * b_im + k_im[..., None] * b_re
    to_gh_p = lambda m: jnp.transpose(m, (0, 2, 1))
    bbig = jnp.concatenate([_block_diag(to_gh_p(bb_re)), _block_diag(to_gh_p(bb_im))], axis=1)
    to_gp_h = lambda m: jnp.transpose(m, (0, 2, 1))
    cbig = jnp.concatenate([_block_diag(to_gp_h(ssm_c_re[l])), -_block_diag(to_gp_h(ssm_c_im[l]))], axis=0)
    row = lambda vec: vec.reshape(1, -1)
    return {
        "gmix": row(mix_norm[l]), "win": w_in[l].astype(BF16),
        "lb_re": row(lb_re), "lb_im": row(lb_im),
        "bbig": bbig.astype(BF16), "cbig": cbig.astype(BF16),
        "d": row(ssm_d[l]), "wglu": ssm_w_glu[l].astype(BF16), "sink": attn_sink[l],
        "convw": conv_w[l], "convb": row(conv_b[l]),
        "wax": jnp.concatenate([_block_diag(lru_w_a[l]), _block_diag(lru_w_x[l])], axis=1).astype(BF16),
        "b_a": row(lru_b_a[l]), "b_x": row(lru_b_x[l]), "lam": row(lru_lambda[l]),
        "gout": row(out_norm[l]), "wout": w_out[l].astype(BF16),
    }


def kernel(x_prompt, x_sample, cache_k, cache_v, state_ssm_re, state_ssm_im, state_conv, state_lru,
           ffn1_norm, ffn1_w_gate, ffn1_w_up, ffn1_w_down, mix_norm, w_in,
           ssm_a_re, ssm_a_im, ssm_log_dt, ssm_b_re, ssm_b_im, ssm_c_re, ssm_c_im, ssm_d, ssm_w_glu,
           attn_sink, conv_w, conv_b, lru_w_a, lru_b_a, lru_w_x, lru_b_x, lru_lambda,
           out_norm, w_out, ffn2_norm, ffn2_w_gate, ffn2_w_up, ffn2_w_down, final_norm):
    bn, L, _ = x_prompt.shape
    nb, ns, _ = x_sample.shape
    row = lambda vec: vec.reshape(1, -1)
    gfin = row(final_norm)

    xp = x_prompt.reshape(bn * L, D_MODEL)
    xs = jnp.transpose(x_sample, (1, 0, 2)).reshape(ns * nb, D_MODEL)
    prompt_states = [[] for _ in range(6)]
    sample_states = [[] for _ in range(6)]
    for l in range(DEPTH):
        p = _layer_params(l, mix_norm, w_in, ssm_a_re, ssm_a_im, ssm_log_dt, ssm_b_re, ssm_b_im, ssm_c_re,
                          ssm_c_im, ssm_d, ssm_w_glu, attn_sink, conv_w, conv_b, lru_w_a, lru_b_a, lru_w_x,
                          lru_b_x, lru_lambda, out_norm, w_out)
        f1 = (row(ffn1_norm[l]), ffn1_w_gate[l], ffn1_w_up[l], ffn1_w_down[l])
        f2 = (row(ffn2_norm[l]), ffn2_w_gate[l], ffn2_w_up[l], ffn2_w_down[l])
        is_last = l == DEPTH - 1

        xp, xs = _ffn(xp, xs, *f1, gfin, False)
        xp3, kp, vp, srp, sip, cvp, lrp = _mixer_prompt(xp.reshape(bn, L, D_MODEL), p)
        xp = xp3.reshape(bn * L, D_MODEL)
        for lst, s in zip(prompt_states, (
                kp.reshape(bn, WINDOW, N_KV, HEAD_DIM), vp.reshape(bn, WINDOW, N_KV, HEAD_DIM),
                srp.reshape(bn, SSM_GROUPS, SSM_STATE), sip.reshape(bn, SSM_GROUPS, SSM_STATE),
                cvp, lrp.reshape(bn, LRU_WIDTH))):
            lst.append(s)

        ck = cache_k[l].reshape(nb, -1, KV_WIDTH)
        cv = cache_v[l].reshape(nb, -1, KV_WIDTH)
        conv0 = jnp.transpose(state_conv[l], (1, 0, 2)).reshape((CONV_WIDTH - 1) * nb, LRU_WIDTH)
        xs, ks, vs, srs, sis, cvs, lrs = _mixer_sample(
            xs, ck, cv, state_ssm_re[l].reshape(nb, SSM_FLAT), state_ssm_im[l].reshape(nb, SSM_FLAT),
            conv0, state_lru[l], p, nb, ns)
        xp, xs = _ffn(xp, xs, *f2, gfin, is_last)
        unflip = lambda m, w: jnp.transpose(m.reshape(-1, nb, w), (1, 0, 2))
        for lst, s in zip(sample_states, (
                unflip(ks, KV_WIDTH).reshape(nb, ns, N_KV, HEAD_DIM),
                unflip(vs, KV_WIDTH).reshape(nb, ns, N_KV, HEAD_DIM),
                srs.reshape(nb, SSM_GROUPS, SSM_STATE), sis.reshape(nb, SSM_GROUPS, SSM_STATE),
                unflip(cvs, LRU_WIDTH), lrs)):
            lst.append(s)

    y_prompt = xp.reshape(bn, L, D_MODEL)
    y_sample = jnp.transpose(xs.reshape(ns, nb, D_MODEL), (1, 0, 2))
    return (y_prompt, y_sample, *[jnp.stack(c, axis=0) for c in prompt_states],
            *[jnp.stack(c, axis=0) for c in sample_states])
```

```python
import functools
import math

import jax
import jax.numpy as jnp
from jax import lax
from jax.experimental import pallas as pl
from jax.experimental.pallas import tpu as pltpu

F32 = jnp.float32
BF16 = jnp.bfloat16

D_MODEL = 1024
D_FF = 2816
DEPTH = 2
CHUNK = 64
SSM_WIDTH = 256
SSM_GROUP = 16
SSM_GROUPS = 16
SSM_STATE = 64
SSM_FLAT = SSM_GROUPS * SSM_STATE
HEAD_DIM = 64
ATT_WIDTH = 512
N_HEADS = 8
N_KV = 2
Q_PER_KV = 4
KV_WIDTH = 128
WINDOW = 128
LRU_WIDTH = 256
LRU_BLOCKS = 4
LRU_BLOCK = 64
CONV_WIDTH = 4
LRU_C = 8.0
OFF_Q = SSM_WIDTH
OFF_K = OFF_Q + ATT_WIDTH
OFF_V = OFF_K + KV_WIDTH
OFF_LX = OFF_V + KV_WIDTH
OFF_LG = OFF_LX + LRU_WIDTH
IN_WIDTH = OFF_LG + LRU_WIDTH
ATT_SCALE = HEAD_DIM ** -0.5
assert math.frexp(ATT_SCALE)[0] == 0.5, "the prompt mixer pre-scales q, exact only for a power-of-two scale"
EPS = 1e-6
NEG_INF = -1e30

SUBLANES = 8
LANES = 128
FFN_ROWS = 512
FFN_COLS = 256
STAGE_SLOTS = 4
STAGE_IN_ROWS = 128
STAGE_OUT_ROWS = 256
MIX_ROWS = 512
SEGS = SUBLANES
SEG_LEN = MIX_ROWS // SEGS
VMEM_LIMIT = 56 * 1024 * 1024


def _rms(x, g):
    return x * lax.rsqrt(jnp.mean(x * x, axis=-1, keepdims=True) + EPS) * g


def _dot(a, b):
    return jnp.dot(a, b, preferred_element_type=F32)


def _cmul(ar, ai, br, bi):
    return ar * br - ai * bi, ar * bi + ai * br


def _put(ref, val):
    for c in range(ref.shape[0]):
        ref[c] = val[:, c * LANES:(c + 1) * LANES]


def _get(ref):
    return jnp.concatenate([ref[c] for c in range(ref.shape[0])], axis=1)


def _get_strided(ref, start, n, stride):
    return jnp.concatenate(
        [ref[c, pl.ds(start, n, stride=stride), :] for c in range(ref.shape[0])], axis=1)


def _put_strided(ref, start, n, stride, val):
    for c in range(ref.shape[0]):
        ref[c, pl.ds(start, n, stride=stride), :] = val[:, c * LANES:(c + 1) * LANES]


def _weight_jobs(layer, wg_hbm, wu_hbm, wd_hbm, wg_v, wu_v, wd_v, stage_in, stage_out, sem):
    jobs = []
    used = [0, 0]

    def add(src, dst, stage, kind, r0, n):
        slot = used[kind] % STAGE_SLOTS
        used[kind] += 1
        copy = pltpu.make_async_copy(src.at[layer, pl.ds(r0, n), :], stage.at[slot], sem.at[kind, slot])
        jobs.append((copy, stage.at[slot], dst, r0, n))

    for r0 in range(0, D_MODEL, STAGE_IN_ROWS):
        add(wg_hbm, wg_v, stage_in, 0, r0, STAGE_IN_ROWS)
        add(wu_hbm, wu_v, stage_in, 0, r0, STAGE_IN_ROWS)
    for r0 in range(0, D_FF, STAGE_OUT_ROWS):
        add(wd_hbm, wd_v, stage_out, 1, r0, STAGE_OUT_ROWS)
    return jobs


def _ffn_kernel(xp_ref, xs_ref, g_ref, wg_hbm, wu_hbm, wd_hbm, gf_ref, op_ref, os_ref,
                wg_ref, wu_ref, wd_ref, stage_in, stage_out, sem, *, layer, final_norm, prompt_tiles):
    i = pl.program_id(0)

    @pl.when(i == 0)
    def _():
        jobs = _weight_jobs(layer, wg_hbm, wu_hbm, wd_hbm, wg_ref, wu_ref, wd_ref, stage_in, stage_out, sem)
        ahead = STAGE_SLOTS - 1
        for k in range(min(ahead, len(jobs))):
            jobs[k][0].start()
        for k, (copy, staged, dst, r0, n) in enumerate(jobs):
            if k + ahead < len(jobs):
                jobs[k + ahead][0].start()
            copy.wait()
            dst[r0:r0 + n, :] = staged[...].astype(BF16)

    x = jnp.where(i < prompt_tiles, xp_ref[...], xs_ref[...])
    h = _rms(x, g_ref[...]).astype(BF16)
    acc = jnp.zeros(x.shape, F32)
    for c in range(D_FF // FFN_COLS):
        sl = slice(c * FFN_COLS, (c + 1) * FFN_COLS)
        g = _dot(h, wg_ref[:, sl])
        u = _dot(h, wu_ref[:, sl])
        a = (g * jax.nn.sigmoid(g) * u).astype(BF16)
        acc = acc + _dot(a, wd_ref[sl, :])
    y = x + 0.5 * acc
    if final_norm:
        y = _rms(y, gf_ref[...])

    @pl.when(i < prompt_tiles)
    def _():
        op_ref[...] = y

    @pl.when(i >= prompt_tiles)
    def _():
        os_ref[...] = y


def _ffn(xp, xs, layer, g, wg, wu, wd, gf, final_norm):
    prompt_tiles = xp.shape[0] // FFN_ROWS
    assert xs.shape[0] == FFN_ROWS
    const = lambda i: (0, 0)
    prompt_map = lambda i: (jnp.minimum(i, prompt_tiles - 1), 0)
    hbm = pl.BlockSpec(memory_space=pl.ANY)
    return pl.pallas_call(
        functools.partial(_ffn_kernel, layer=layer, final_norm=final_norm, prompt_tiles=prompt_tiles),
        grid=(prompt_tiles + 1,),
        in_specs=[
            pl.BlockSpec((FFN_ROWS, D_MODEL), prompt_map),
            pl.BlockSpec((FFN_ROWS, D_MODEL), const),
            pl.BlockSpec((1, D_MODEL), const),
            hbm, hbm, hbm,
            pl.BlockSpec((1, D_MODEL), const),
        ],
        out_specs=[pl.BlockSpec((FFN_ROWS, D_MODEL), prompt_map), pl.BlockSpec((FFN_ROWS, D_MODEL), const)],
        out_shape=[jax.ShapeDtypeStruct(xp.shape, F32), jax.ShapeDtypeStruct(xs.shape, F32)],
        scratch_shapes=[
            pltpu.VMEM((D_MODEL, D_FF), BF16), pltpu.VMEM((D_MODEL, D_FF), BF16), pltpu.VMEM((D_FF, D_MODEL), BF16),
            pltpu.VMEM((STAGE_SLOTS, STAGE_IN_ROWS, D_FF), F32),
            pltpu.VMEM((STAGE_SLOTS, STAGE_OUT_ROWS, D_MODEL), F32),
            pltpu.SemaphoreType.DMA((2, STAGE_SLOTS)),
        ],
        compiler_params=pltpu.CompilerParams(
            dimension_semantics=("arbitrary",), vmem_limit_bytes=VMEM_LIMIT),
        name="ffn",
    )(xp, xs, g, wg, wu, wd, gf)


def _log_sigmoid(x):
    return jnp.minimum(x, 0.0) - jnp.log1p(jnp.exp(-jnp.abs(x)))


def _s5_post(y, u, d, wglu, g):
    z = jax.nn.gelu(y + d * u)
    out = z * jax.nn.sigmoid(_dot(z.astype(BF16), wglu))
    return _rms(out, g)


def _lru_gates(xc, wax, b_a, b_x, lam):
    ga = _dot(xc.astype(BF16), wax)
    r = jax.nn.sigmoid(ga[:, :LRU_WIDTH] + b_a)
    i = jax.nn.sigmoid(ga[:, LRU_WIDTH:] + b_x)
    log_a = LRU_C * r * _log_sigmoid(lam)
    a = jnp.exp(log_a)
    t = jnp.tanh(log_a)
    mult = jnp.sqrt(-2.0 * t / (1.0 - t))
    return a, mult * i * xc


def _sink_softmax(s, sink_col):
    m = jnp.maximum(jnp.max(s, axis=-1, keepdims=True), sink_col)
    e = jnp.exp(s - m)
    den = jnp.sum(e, axis=-1, keepdims=True) + jnp.exp(sink_col - m)
    return e / den


def _attend(q_rows, kk, vv, sink_ref, nq, valid):
    outs = []
    for kvh in range(N_KV):
        qs = jnp.concatenate(
            [q_rows[:, (kvh * Q_PER_KV + g) * HEAD_DIM:(kvh * Q_PER_KV + g + 1) * HEAD_DIM]
             for g in range(Q_PER_KV)], axis=0)
        kh = kk[:, kvh * HEAD_DIM:(kvh + 1) * HEAD_DIM]
        vh = vv[:, kvh * HEAD_DIM:(kvh + 1) * HEAD_DIM]
        s = lax.dot_general(qs, kh, (((1,), (1,)), ((), ())), preferred_element_type=F32) * ATT_SCALE
        if valid is not None:
            s = jnp.where(valid, s, NEG_INF)
        sink_col = jnp.concatenate(
            [jnp.full((nq, 1), sink_ref[kvh * Q_PER_KV + g], F32) for g in range(Q_PER_KV)], axis=0)
        p = _sink_softmax(s, sink_col).astype(BF16)
        o = _dot(p, vh)
        outs.extend(o[g * nq:(g + 1) * nq, :] for g in range(Q_PER_KV))
    return jnp.concatenate(outs, axis=1)


def _mix_out(x, ys, ya, yl, gout, wout):
    ycat = jnp.concatenate([
        ys,
        _rms(ya, gout[:, SSM_WIDTH:SSM_WIDTH + ATT_WIDTH]),
        _rms(yl, gout[:, SSM_WIDTH + ATT_WIDTH:]),
    ], axis=1).astype(BF16)
    return x + _dot(ycat, wout)


def _mixer_prompt_kernel(
        x_ref, gmix_ref, win_ref, lbre_ref, lbim_ref, bbig_ref, cbig_ref, d_ref, wglu_ref, sink_ref,
        convw_ref, convb_ref, wax_ref, ba_ref, bx_ref, lam_ref, gout_ref, wout_ref, unperm_ref,
        o_ref, kout_ref, vout_ref, sre_ref, sim_ref, convout_ref, lruout_ref,
        st_re, st_im, st_lru, u_buf, up_buf, xs_buf, hs_buf, xl_buf, a_buf, b_buf, hl_buf, lb_buf,
        qz_buf, k_buf, vt_buf, yt_buf):
    j = pl.program_id(1)
    last = pl.num_programs(1) - 1
    T = MIX_ROWS

    @pl.when(j == 0)
    def _():
        st_re[...] = jnp.zeros_like(st_re)
        st_im[...] = jnp.zeros_like(st_im)
        st_lru[...] = jnp.zeros_like(st_lru)
        xl_buf[0:SUBLANES, :] = jnp.zeros((SUBLANES, LRU_WIDTH), F32)
        k_buf[0:WINDOW, :] = jnp.zeros((WINDOW, KV_WIDTH), BF16)
        vt_buf[0] = jnp.zeros((KV_WIDTH, WINDOW), BF16)

    x = x_ref[0]
    h = _rms(x, gmix_ref[...]).astype(BF16)
    proj = _dot(h, win_ref[...])
    u = proj[:, :OFF_Q]
    k = proj[:, OFF_K:OFF_V]
    v = proj[:, OFF_V:OFF_LX]
    xl = proj[:, OFF_LX:OFF_LG]
    gl = proj[:, OFF_LG:]
    gout = gout_ref[...]

    _put(u_buf, u)
    for i in range(SEG_LEN):
        up_buf[i * SEGS:(i + 1) * SEGS, :] = _get_strided(u_buf, i, SEGS, SEG_LEN)
    up = up_buf[...]
    xs_buf[...] = _dot(up.astype(BF16), bbig_ref[...])
    lr = lbre_ref[...]
    li = lbim_ref[...]
    lb_buf[0] = jnp.broadcast_to(lr, (SEGS, SSM_FLAT))
    lb_buf[1] = jnp.broadcast_to(li, (SEGS, SSM_FLAT))

    def s5_two_steps(ii, carry, store):
        hr, hi = carry
        r0 = pl.multiple_of(ii * 2 * SEGS, 2 * SEGS)
        lrb = lb_buf[0]
        lib = lb_buf[1]
        new_r, new_i = [], []
        for half in range(2):
            xr = xs_buf[pl.ds(r0 + half * SEGS, SEGS), 0:SSM_FLAT]
            xi = xs_buf[pl.ds(r0 + half * SEGS, SEGS), SSM_FLAT:2 * SSM_FLAT]
            hr, hi = lrb * hr - lib * hi + xr, lrb * hi + lib * hr + xi
            new_r.append(hr)
            new_i.append(hi)
        if store:
            hs_buf[pl.ds(r0, 2 * SEGS), 0:SSM_FLAT] = jnp.concatenate(new_r, axis=0).astype(BF16)
            hs_buf[pl.ds(r0, 2 * SEGS), SSM_FLAT:2 * SSM_FLAT] = jnp.concatenate(new_i, axis=0).astype(BF16)
        return hr, hi

    zero_seg = jnp.zeros((SEGS, SSM_FLAT), F32)
    loc_r, loc_i = lax.fori_loop(0, SEG_LEN // 2, functools.partial(s5_two_steps, store=False),
                                 (zero_seg, zero_seg))
    pr, pi = lr, li
    for _ in range(int(math.log2(SEG_LEN))):
        pr, pi = _cmul(pr, pi, pr, pi)
    gr, gi = st_re[...], st_im[...]
    rows_r, rows_i = [], []
    for s in range(SEGS):
        rows_r.append(gr)
        rows_i.append(gi)
        ar, ai = _cmul(pr, pi, gr, gi)
        gr = ar + loc_r[s:s + 1, :]
        gi = ai + loc_i[s:s + 1, :]
    st_re[...] = gr
    st_im[...] = gi
    lax.fori_loop(0, SEG_LEN // 2, functools.partial(s5_two_steps, store=True),
                  (jnp.concatenate(rows_r, axis=0), jnp.concatenate(rows_i, axis=0)))
    y = _dot(hs_buf[...], cbig_ref[...])
    ys_regrouped = _s5_post(y, up, d_ref[...], wglu_ref[...], gout[:, :SSM_WIDTH]).astype(BF16)
    ys = _dot(unperm_ref[...], ys_regrouped)

    xl_buf[SUBLANES:SUBLANES + T, :] = xl
    cw = convw_ref[...]
    xc = convb_ref[...] + cw[0:1, :] * xl_buf[SUBLANES - 3:SUBLANES - 3 + T, :]
    for t in range(1, CONV_WIDTH):
        xc = xc + cw[t:t + 1, :] * xl_buf[SUBLANES - 3 + t:SUBLANES - 3 + t + T, :]
    tail = xl_buf[T:T + SUBLANES, :]
    xl_buf[0:SUBLANES, :] = tail
    a, b = _lru_gates(xc, wax_ref[...], ba_ref[...], bx_ref[...], lam_ref[...])
    a3 = a.reshape(T // SUBLANES, SUBLANES, LRU_WIDTH)
    b3 = b.reshape(T // SUBLANES, SUBLANES, LRU_WIDTH)
    slab_row = lax.broadcasted_iota(jnp.int32, a3.shape, 1)
    shift = 1
    while shift < SUBLANES:
        keep = slab_row >= shift
        b3 = jnp.where(keep, a3 * pltpu.roll(b3, shift, axis=1) + b3, b3)
        a3 = jnp.where(keep, a3 * pltpu.roll(a3, shift, axis=1), a3)
        shift *= 2
    a_buf[...] = a3.reshape(T, LRU_WIDTH)
    b_buf[...] = b3.reshape(T, LRU_WIDTH)

    def lru_slab(i, carry):
        r0 = pl.multiple_of(i * SUBLANES, SUBLANES)
        hh = a_buf[pl.ds(r0, SUBLANES), :] * carry + b_buf[pl.ds(r0, SUBLANES), :]
        hl_buf[pl.ds(r0, SUBLANES), :] = hh
        return jnp.broadcast_to(hh[SUBLANES - 1:SUBLANES, :], (SUBLANES, LRU_WIDTH))

    lru_carry = lax.fori_loop(0, T // SUBLANES, lru_slab,
                              jnp.broadcast_to(st_lru[...], (SUBLANES, LRU_WIDTH)), unroll=4)
    gl_state = lru_carry[0:1, :]
    st_lru[...] = gl_state
    yl = jax.nn.gelu(gl) * hl_buf[...]

    lo = lax.broadcasted_iota(jnp.int32, (T, LANES), 1) < HEAD_DIM
    for jp in range(N_HEADS // 2):
        pair = proj[:, OFF_Q + jp * LANES:OFF_Q + (jp + 1) * LANES] * ATT_SCALE
        swapped = pltpu.roll(pair, HEAD_DIM, axis=1)
        if jp < N_HEADS // 4:
            even, odd = jnp.where(lo, pair, 0.0), jnp.where(lo, swapped, 0.0)
        else:
            even, odd = jnp.where(lo, 0.0, swapped), jnp.where(lo, 0.0, pair)
        qz_buf[2 * jp] = even.astype(BF16)
        qz_buf[2 * jp + 1] = odd.astype(BF16)
    k_buf[WINDOW:WINDOW + T, :] = k.astype(BF16)
    vt = v.T.astype(BF16)
    for i in range(T // WINDOW):
        vt_buf[i + 1] = vt[:, i * WINDOW:(i + 1) * WINDOW]

    span = 2 * CHUNK
    krow = lax.broadcasted_iota(jnp.int32, (2 * span, 2 * span), 0)
    qcol = lax.broadcasted_iota(jnp.int32, (2 * span, 2 * span), 1) % span
    band = jnp.logical_or(jnp.logical_and(qcol < CHUNK, krow < WINDOW + CHUNK),
                          jnp.logical_and(qcol >= CHUNK, krow >= CHUNK))
    for m in range(T // span):
        valid = band if m > 0 else jnp.logical_and(band, jnp.logical_or(krow >= WINDOW, j > 0))
        top, bot = (CHUNK if m > 0 else WINDOW), WINDOW + CHUNK
        kwin = k_buf[m * span:(m + 2) * span, :]
        vtwin = jnp.concatenate([vt_buf[m], vt_buf[m + 1]], axis=1)
        for hp in range(N_HEADS // 2):
            kvh = hp // (Q_PER_KV // 2)
            qz = jnp.concatenate([qz_buf[2 * hp, m * span:(m + 1) * span, :],
                                  qz_buf[2 * hp + 1, m * span:(m + 1) * span, :]], axis=0)
            s = lax.dot_general(kwin, qz, (((1,), (1,)), ((), ())), preferred_element_type=F32)
            s = jnp.concatenate([jnp.where(valid[:top], s[:top], NEG_INF), s[top:bot],
                                 jnp.where(valid[bot:], s[bot:], NEG_INF)], axis=0)
            sink_row = jnp.concatenate([jnp.full((1, span), sink_ref[2 * hp], F32),
                                        jnp.full((1, span), sink_ref[2 * hp + 1], F32)], axis=1)
            mx = jnp.maximum(jnp.max(s, axis=0, keepdims=True), sink_row)
            e = jnp.exp(s - mx)
            den = jnp.sum(e, axis=0, keepdims=True) + jnp.exp(sink_row - mx)
            o = _dot(vtwin, e.astype(BF16))[kvh * HEAD_DIM:(kvh + 1) * HEAD_DIM, :] * (1.0 / den)
            yt_buf[2 * hp * HEAD_DIM:(2 * hp + 1) * HEAD_DIM, m * span:(m + 1) * span] = o[:, :span]
            yt_buf[(2 * hp + 1) * HEAD_DIM:(2 * hp + 2) * HEAD_DIM, m * span:(m + 1) * span] = o[:, span:]
    k_buf[0:WINDOW, :] = k_buf[T:T + WINDOW, :]
    vt_buf[0] = vt_buf[T // WINDOW]

    o_ref[0] = _mix_out(x, ys, yt_buf[...].T, yl, gout, wout_ref[...])

    @pl.when(j == last)
    def _():
        kout_ref[0] = k[T - WINDOW:, :]
        vout_ref[0] = v[T - WINDOW:, :]
        sre_ref[0] = gr
        sim_ref[0] = gi
        convout_ref[0] = tail[SUBLANES - (CONV_WIDTH - 1):, :]
        lruout_ref[0] = gl_state


def _mixer_prompt(x, p):
    bn, L, _ = x.shape
    T = MIX_ROWS
    const2 = lambda b, j: (0, 0)
    vspec = lambda shape: pl.BlockSpec(shape, const2)
    per_b = lambda shape: pl.BlockSpec((1,) + shape, lambda b, j: (b, 0, 0))
    in_specs = [
        pl.BlockSpec((1, T, D_MODEL), lambda b, j: (b, j, 0)),
        vspec((1, D_MODEL)), vspec((D_MODEL, IN_WIDTH)),
        vspec((1, SSM_FLAT)), vspec((1, SSM_FLAT)),
        vspec((SSM_WIDTH, 2 * SSM_FLAT)), vspec((2 * SSM_FLAT, SSM_WIDTH)),
        vspec((1, SSM_WIDTH)), vspec((SSM_WIDTH, SSM_WIDTH)),
        pl.BlockSpec(memory_space=pltpu.SMEM),
        vspec((CONV_WIDTH, LRU_WIDTH)), vspec((1, LRU_WIDTH)),
        vspec((LRU_WIDTH, 2 * LRU_WIDTH)), vspec((1, LRU_WIDTH)), vspec((1, LRU_WIDTH)), vspec((1, LRU_WIDTH)),
        vspec((1, D_MODEL)), vspec((D_MODEL, D_MODEL)), vspec((T, T)),
    ]
    out_specs = [
        pl.BlockSpec((1, T, D_MODEL), lambda b, j: (b, j, 0)),
        per_b((WINDOW, KV_WIDTH)), per_b((WINDOW, KV_WIDTH)),
        per_b((1, SSM_FLAT)), per_b((1, SSM_FLAT)),
        per_b((CONV_WIDTH - 1, LRU_WIDTH)), per_b((1, LRU_WIDTH)),
    ]
    out_shape = [
        jax.ShapeDtypeStruct((bn, L, D_MODEL), F32),
        jax.ShapeDtypeStruct((bn, WINDOW, KV_WIDTH), F32), jax.ShapeDtypeStruct((bn, WINDOW, KV_WIDTH), F32),
        jax.ShapeDtypeStruct((bn, 1, SSM_FLAT), F32), jax.ShapeDtypeStruct((bn, 1, SSM_FLAT), F32),
        jax.ShapeDtypeStruct((bn, CONV_WIDTH - 1, LRU_WIDTH), F32), jax.ShapeDtypeStruct((bn, 1, LRU_WIDTH), F32),
    ]
    planes = lambda width: pltpu.VMEM((width // LANES, T, LANES), F32)
    scratch = [
        pltpu.VMEM((1, SSM_FLAT), F32), pltpu.VMEM((1, SSM_FLAT), F32), pltpu.VMEM((1, LRU_WIDTH), F32),
        planes(SSM_WIDTH), pltpu.VMEM((T, SSM_WIDTH), F32),
        pltpu.VMEM((T, 2 * SSM_FLAT), F32), pltpu.VMEM((T, 2 * SSM_FLAT), BF16),
        pltpu.VMEM((T + SUBLANES, LRU_WIDTH), F32),
        pltpu.VMEM((T, LRU_WIDTH), F32), pltpu.VMEM((T, LRU_WIDTH), F32), pltpu.VMEM((T, LRU_WIDTH), F32),
        pltpu.VMEM((2, SEGS, SSM_FLAT), F32),
        pltpu.VMEM((N_HEADS, T, LANES), BF16),
        pltpu.VMEM((T + WINDOW, KV_WIDTH), BF16),
        pltpu.VMEM((T // WINDOW + 1, KV_WIDTH, WINDOW), BF16),
        pltpu.VMEM((ATT_WIDTH, T), F32),
    ]
    return pl.pallas_call(
        _mixer_prompt_kernel,
        grid=(bn, L // T),
        in_specs=in_specs, out_specs=out_specs, out_shape=out_shape, scratch_shapes=scratch,
        compiler_params=pltpu.CompilerParams(
            dimension_semantics=("arbitrary", "arbitrary"), vmem_limit_bytes=VMEM_LIMIT),
        name="mixer_prompt",
    )(x, p["gmix"], p["win"], p["lb_re"], p["lb_im"], p["bbig"], p["cbig"], p["d"], p["wglu"], p["sink"],
      p["convw"], p["convb"], p["wax"], p["b_a"], p["b_x"], p["lam"], p["gout"], p["wout"], _unpermute_matrix())


def _unpermute_matrix():
    token = jnp.arange(MIX_ROWS)
    source = (token % SEG_LEN) * SEGS + token // SEG_LEN
    return (source[:, None] == jnp.arange(MIX_ROWS)[None, :]).astype(BF16)


def _mixer_sample_kernel(
        x_ref, ck_ref, cv_ref, s0re_ref, s0im_ref, conv0_ref, lru0_ref,
        gmix_ref, win_ref, lbre_ref, lbim_ref, bbig_ref, cbig_ref, d_ref, wglu_ref, sink_ref,
        convw_ref, convb_ref, wax_ref, ba_ref, bx_ref, lam_ref, gout_ref, wout_ref,
        o_ref, kout_ref, vout_ref, sre_ref, sim_ref, convout_ref, lruout_ref,
        xs_buf, xp_buf, a_buf, b_buf, hl_buf, q_buf, ya_buf, *, nb, ns):
    rows = nb * ns
    x = x_ref[...]
    h = _rms(x, gmix_ref[...]).astype(BF16)
    proj = _dot(h, win_ref[...])
    u = proj[:, :OFF_Q]
    k = proj[:, OFF_K:OFF_V]
    v = proj[:, OFF_V:OFF_LX]
    xl = proj[:, OFF_LX:OFF_LG]
    gl = proj[:, OFF_LG:]
    gout = gout_ref[...]
    kout_ref[...] = k
    vout_ref[...] = v

    xs_buf[...] = _dot(u.astype(BF16), bbig_ref[...])
    lr = lbre_ref[...]
    li = lbim_ref[...]
    hr = s0re_ref[...]
    hi = s0im_ref[...]
    for t in range(ns):
        sl = slice(t * nb, (t + 1) * nb)
        nr = lr * hr - li * hi + xs_buf[sl, 0:SSM_FLAT]
        ni = lr * hi + li * hr + xs_buf[sl, SSM_FLAT:2 * SSM_FLAT]
        hr, hi = nr, ni
        xs_buf[sl, 0:SSM_FLAT] = hr
        xs_buf[sl, SSM_FLAT:2 * SSM_FLAT] = hi
    sre_ref[...] = hr
    sim_ref[...] = hi
    y = _dot(xs_buf[...].astype(BF16), cbig_ref[...])
    ys = _s5_post(y, u, d_ref[...], wglu_ref[...], gout[:, :SSM_WIDTH])

    npre = (CONV_WIDTH - 1) * nb
    xp_buf[0:npre, :] = conv0_ref[...]
    xp_buf[npre:npre + rows, :] = xl
    cw = convw_ref[...]
    xc = convb_ref[...] + cw[0:1, :] * xp_buf[0:rows, :]
    for t in range(1, CONV_WIDTH):
        xc = xc + cw[t:t + 1, :] * xp_buf[t * nb:t * nb + rows, :]
    convout_ref[...] = xp_buf[rows:rows + npre, :]
    a, b = _lru_gates(xc, wax_ref[...], ba_ref[...], bx_ref[...], lam_ref[...])
    a_buf[...] = a
    b_buf[...] = b
    hh = lru0_ref[...]
    for t in range(ns):
        sl = slice(t * nb, (t + 1) * nb)
        hh = a_buf[sl, :] * hh + b_buf[sl, :]
        hl_buf[sl, :] = hh
    lruout_ref[...] = hh
    yl = jax.nn.gelu(gl) * hl_buf[...]

    _put(q_buf, proj[:, OFF_Q:OFF_K])

    def attn_stream(bi, _):
        qb = _get_strided(q_buf, bi, ns, nb).astype(BF16)
        kn = kout_ref[pl.ds(bi, ns, stride=nb), :]
        vn = vout_ref[pl.ds(bi, ns, stride=nb), :]
        kk = jnp.concatenate([ck_ref[bi], kn], axis=0).astype(BF16)
        vv = jnp.concatenate([cv_ref[bi], vn], axis=0).astype(BF16)
        _put_strided(ya_buf, bi, ns, nb, _attend(qb, kk, vv, sink_ref, ns, None))
        return 0

    lax.fori_loop(0, nb, attn_stream, 0)
    o_ref[...] = _mix_out(x, ys, _get(ya_buf), yl, gout, wout_ref[...])


def _mixer_sample(x, ck, cv, s0re, s0im, conv0, lru0, p, nb, ns):
    rows = nb * ns
    vm = pl.BlockSpec(memory_space=pltpu.VMEM)
    in_specs = [vm] * 15 + [pl.BlockSpec(memory_space=pltpu.SMEM)] + [vm] * 8
    out_shape = [
        jax.ShapeDtypeStruct((rows, D_MODEL), F32),
        jax.ShapeDtypeStruct((rows, KV_WIDTH), F32), jax.ShapeDtypeStruct((rows, KV_WIDTH), F32),
        jax.ShapeDtypeStruct((nb, SSM_FLAT), F32), jax.ShapeDtypeStruct((nb, SSM_FLAT), F32),
        jax.ShapeDtypeStruct(((CONV_WIDTH - 1) * nb, LRU_WIDTH), F32), jax.ShapeDtypeStruct((nb, LRU_WIDTH), F32),
    ]
    scratch = [
        pltpu.VMEM((rows, 2 * SSM_FLAT), F32),
        pltpu.VMEM((rows + (CONV_WIDTH - 1) * nb, LRU_WIDTH), F32),
        pltpu.VMEM((rows, LRU_WIDTH), F32), pltpu.VMEM((rows, LRU_WIDTH), F32), pltpu.VMEM((rows, LRU_WIDTH), F32),
        pltpu.VMEM((ATT_WIDTH // LANES, rows, LANES), F32),
        pltpu.VMEM((ATT_WIDTH // LANES, rows, LANES), F32),
    ]
    return pl.pallas_call(
        functools.partial(_mixer_sample_kernel, nb=nb, ns=ns),
        in_specs=in_specs, out_specs=[vm] * 7, out_shape=out_shape, scratch_shapes=scratch,
        compiler_params=pltpu.CompilerParams(vmem_limit_bytes=VMEM_LIMIT),
        name="mixer_sample",
    )(x, ck, cv, s0re, s0im, conv0, lru0,
      p["gmix"], p["win"], p["lb_re"], p["lb_im"], p["bbig"], p["cbig"], p["d"], p["wglu"], p["sink"],
      p["convw"], p["convb"], p["wax"], p["b_a"], p["b_x"], p["lam"], p["gout"], p["wout"])


def _block_diag(w):
    n, i, o = w.shape
    eye = jnp.eye(n, dtype=w.dtype)
    return jnp.einsum("nm,nio->nimo", eye, w).reshape(n * i, n * o)


def _layer_params(l, mix_norm, w_in, ssm_a_re, ssm_a_im, ssm_log_dt, ssm_b_re, ssm_b_im, ssm_c_re, ssm_c_im,
                  ssm_d, ssm_w_glu, attn_sink, conv_w, conv_b, lru_w_a, lru_b_a, lru_w_x, lru_b_x, lru_lambda,
                  out_norm, w_out):
    a_re = ssm_a_re[l]
    a_im = ssm_a_im[l]
    dt = jnp.exp(ssm_log_dt[l])[:, None]
    mag = jnp.exp(a_re * dt)
    lb_re = mag * jnp.cos(a_im * dt)
    lb_im = mag * jnp.sin(a_im * dt)
    den = a_re * a_re + a_im * a_im
    nr = lb_re - 1.0
    k_re = (nr * a_re + lb_im * a_im) / den
    k_im = (lb_im * a_re - nr * a_im) / den
    b_re = ssm_b_re[l]
    b_im = ssm_b_im[l]
    bb_re = k_re[..., None] * b_re - k_im[..., None] * b_im
    bb_im = k_re[..., None] * b_im + k_im[..., None] * b_re
    to_gh_p = lambda m: jnp.transpose(m, (0, 2, 1))
    bbig = jnp.concatenate([_block_diag(to_gh_p(bb_re)), _block_diag(to_gh_p(bb_im))], axis=1)
    to_gp_h = lambda m: jnp.transpose(m, (0, 2, 1))
    cbig = jnp.concatenate([_block_diag(to_gp_h(ssm_c_re[l])), -_block_diag(to_gp_h(ssm_c_im[l]))], axis=0)
    row = lambda vec: vec.reshape(1, -1)
    return {
        "gmix": row(mix_norm[l]), "win": w_in[l].astype(BF16),
        "lb_re": row(lb_re), "lb_im": row(lb_im),
        "bbig": bbig.astype(BF16), "cbig": cbig.astype(BF16),
        "d": row(ssm_d[l]), "wglu": ssm_w_glu[l].astype(BF16), "sink": attn_sink[l],
        "convw": conv_w[l], "convb": row(conv_b[l]),
        "wax": jnp.concatenate([_block_diag(lru_w_a[l]), _block_diag(lru_w_x[l])], axis=1).astype(BF16),
        "b_a": row(lru_b_a[l]), "b_x": row(lru_b_x[l]), "lam": row(lru_lambda[l]),
        "gout": row(out_norm[l]), "wout": w_out[l].astype(BF16),
    }


def kernel(x_prompt, x_sample, cache_k, cache_v, state_ssm_re, state_ssm_im, state_conv, state_lru,
           ffn1_norm, ffn1_w_gate, ffn1_w_up, ffn1_w_down, mix_norm, w_in,
           ssm_a_re, ssm_a_im, ssm_log_dt, ssm_b_re, ssm_b_im, ssm_c_re, ssm_c_im, ssm_d, ssm_w_glu,
           attn_sink, conv_w, conv_b, lru_w_a, lru_b_a, lru_w_x, lru_b_x, lru_lambda,
           out_norm, w_out, ffn2_norm, ffn2_w_gate, ffn2_w_up, ffn2_w_down, final_norm):
    bn, L, _ = x_prompt.shape
    nb, ns, _ = x_sample.shape
    row = lambda vec: vec.reshape(1, -1)
    gfin = row(final_norm)

    xp = x_prompt.reshape(bn * L, D_MODEL)
    xs = jnp.transpose(x_sample, (1, 0, 2)).reshape(ns * nb, D_MODEL)
    prompt_states = [[] for _ in range(6)]
    sample_states = [[] for _ in range(6)]
    for l in range(DEPTH):
        p = _layer_params(l, mix_norm, w_in, ssm_a_re, ssm_a_im, ssm_log_dt, ssm_b_re, ssm_b_im, ssm_c_re,
                          ssm_c_im, ssm_d, ssm_w_glu, attn_sink, conv_w, conv_b, lru_w_a, lru_b_a, lru_w_x,
                          lru_b_x, lru_lambda, out_norm, w_out)
        f1 = (l, row(ffn1_norm[l]), ffn1_w_gate, ffn1_w_up, ffn1_w_down)
        f2 = (l, row(ffn2_norm[l]), ffn2_w_gate, ffn2_w_up, ffn2_w_down)
        is_last = l == DEPTH - 1

        xp, xs = _ffn(xp, xs, *f1, gfin, False)
        xp3, kp, vp, srp, sip, cvp, lrp = _mixer_prompt(xp.reshape(bn, L, D_MODEL), p)
        xp = xp3.reshape(bn * L, D_MODEL)
        for lst, s in zip(prompt_states, (
                kp.reshape(bn, WINDOW, N_KV, HEAD_DIM), vp.reshape(bn, WINDOW, N_KV, HEAD_DIM),
                srp.reshape(bn, SSM_GROUPS, SSM_STATE), sip.reshape(bn, SSM_GROUPS, SSM_STATE),
                cvp, lrp.reshape(bn, LRU_WIDTH))):
            lst.append(s)

        ck = cache_k[l].reshape(nb, -1, KV_WIDTH)
        cv = cache_v[l].reshape(nb, -1, KV_WIDTH)
        conv0 = jnp.transpose(state_conv[l], (1, 0, 2)).reshape((CONV_WIDTH - 1) * nb, LRU_WIDTH)
        xs, ks, vs, srs, sis, cvs, lrs = _mixer_sample(
            xs, ck, cv, state_ssm_re[l].reshape(nb, SSM_FLAT), state_ssm_im[l].reshape(nb, SSM_FLAT),
            conv0, state_lru[l], p, nb, ns)
        xp, xs = _ffn(xp, xs, *f2, gfin, is_last)
        unflip = lambda m, w: jnp.transpose(m.reshape(-1, nb, w), (1, 0, 2))
        for lst, s in zip(sample_states, (
                unflip(ks, KV_WIDTH).reshape(nb, ns, N_KV, HEAD_DIM),
                unflip(vs, KV_WIDTH).reshape(nb, ns, N_KV, HEAD_DIM),
                srs.reshape(nb, SSM_GROUPS, SSM_STATE), sis.reshape(nb, SSM_GROUPS, SSM_STATE),
                unflip(cvs, LRU_WIDTH), lrs)):
            lst.append(s)

    y_prompt = xp.reshape(bn, L, D_MODEL)
    y_sample = jnp.transpose(xs.reshape(ns, nb, D_MODEL), (1, 0, 2))
    return (y_prompt, y_sample, *[jnp.stack(c, axis=0) for c in prompt_states],
            *[jnp.stack(c, axis=0) for c in sample_states])
```

```python
import functools
import math

import jax
import jax.numpy as jnp
from jax import lax
from jax.experimental import pallas as pl
from jax.experimental.pallas import tpu as pltpu

F32 = jnp.float32
BF16 = jnp.bfloat16

D_MODEL = 1024
D_FF = 2816
DEPTH = 2
CHUNK = 64
SSM_WIDTH = 256
SSM_GROUP = 16
SSM_GROUPS = 16
SSM_STATE = 64
SSM_FLAT = SSM_GROUPS * SSM_STATE
HEAD_DIM = 64
ATT_WIDTH = 512
N_HEADS = 8
N_KV = 2
Q_PER_KV = 4
KV_WIDTH = 128
WINDOW = 128
LRU_WIDTH = 256
LRU_BLOCKS = 4
LRU_BLOCK = 64
CONV_WIDTH = 4
LRU_C = 8.0
OFF_Q = SSM_WIDTH
OFF_K = OFF_Q + ATT_WIDTH
OFF_V = OFF_K + KV_WIDTH
OFF_LX = OFF_V + KV_WIDTH
OFF_LG = OFF_LX + LRU_WIDTH
IN_WIDTH = OFF_LG + LRU_WIDTH
ATT_SCALE = HEAD_DIM ** -0.5
assert math.frexp(ATT_SCALE)[0] == 0.5, "the prompt mixer pre-scales q, exact only for a power-of-two scale"
EPS = 1e-6
NEG_INF = -1e30

SUBLANES = 8
LANES = 128
FFN_ROWS = 512
FFN_COLS = 256
STAGE_SLOTS = 4
STAGE_IN_ROWS = 128
STAGE_OUT_ROWS = 256
MIX_ROWS = 512
SEGS = SUBLANES
SEG_LEN = MIX_ROWS // SEGS
VMEM_LIMIT = 56 * 1024 * 1024


def _rms(x, g):
    return x * lax.rsqrt(jnp.mean(x * x, axis=-1, keepdims=True) + EPS) * g


def _dot(a, b):
    return jnp.dot(a, b, preferred_element_type=F32)


def _cmul(ar, ai, br, bi):
    return ar * br - ai * bi, ar * bi + ai * br


def _put(ref, val):
    for c in range(ref.shape[0]):
        ref[c] = val[:, c * LANES:(c + 1) * LANES]


def _get(ref):
    return jnp.concatenate([ref[c] for c in range(ref.shape[0])], axis=1)


def _get_strided(ref, start, n, stride):
    return jnp.concatenate(
        [ref[c, pl.ds(start, n, stride=stride), :] for c in range(ref.shape[0])], axis=1)


def _put_strided(ref, start, n, stride, val):
    for c in range(ref.shape[0]):
        ref[c, pl.ds(start, n, stride=stride), :] = val[:, c * LANES:(c + 1) * LANES]


def _weight_jobs(layer, wg_hbm, wu_hbm, wd_hbm, wg_v, wu_v, wd_v, stage_in, stage_out, sem):
    jobs = []
    used = [0, 0]

    def add(src, dst, stage, kind, r0, n):
        slot = used[kind] % STAGE_SLOTS
        used[kind] += 1
        copy = pltpu.make_async_copy(src.at[layer, pl.ds(r0, n), :], stage.at[slot], sem.at[kind, slot])
        jobs.append((copy, stage.at[slot], dst, r0, n))

    for r0 in range(0, D_MODEL, STAGE_IN_ROWS):
        add(wg_hbm, wg_v, stage_in, 0, r0, STAGE_IN_ROWS)
        add(wu_hbm, wu_v, stage_in, 0, r0, STAGE_IN_ROWS)
    for r0 in range(0, D_FF, STAGE_OUT_ROWS):
        add(wd_hbm, wd_v, stage_out, 1, r0, STAGE_OUT_ROWS)
    return jobs


def _ffn_kernel(xp_ref, xs_ref, g_ref, wg_hbm, wu_hbm, wd_hbm, gf_ref, op_ref, os_ref,
                wg_ref, wu_ref, wd_ref, stage_in, stage_out, sem, *, layer, final_norm, prompt_tiles):
    i = pl.program_id(0)

    @pl.when(i == 0)
    def _():
        jobs = _weight_jobs(layer, wg_hbm, wu_hbm, wd_hbm, wg_ref, wu_ref, wd_ref, stage_in, stage_out, sem)
        ahead = STAGE_SLOTS - 1
        for k in range(min(ahead, len(jobs))):
            jobs[k][0].start()
        for k, (copy, staged, dst, r0, n) in enumerate(jobs):
            if k + ahead < len(jobs):
                jobs[k + ahead][0].start()
            copy.wait()
            dst[r0:r0 + n, :] = staged[...].astype(BF16)

    x = jnp.where(i < prompt_tiles, xp_ref[...], xs_ref[...])
    h = _rms(x, g_ref[...]).astype(BF16)
    acc = jnp.zeros(x.shape, F32)
    for c in range(D_FF // FFN_COLS):
        sl = slice(c * FFN_COLS, (c + 1) * FFN_COLS)
        g = _dot(h, wg_ref[:, sl])
        u = _dot(h, wu_ref[:, sl])
        a = (g * jax.nn.sigmoid(g) * u).astype(BF16)
        acc = acc + _dot(a, wd_ref[sl, :])
    y = x + 0.5 * acc
    if final_norm:
        y = _rms(y, gf_ref[...])

    @pl.when(i < prompt_tiles)
    def _():
        op_ref[...] = y

    @pl.when(i >= prompt_tiles)
    def _():
        os_ref[...] = y


def _ffn(xp, xs, layer, g, wg, wu, wd, gf, final_norm):
    prompt_tiles = xp.shape[0] // FFN_ROWS
    assert xs.shape[0] == FFN_ROWS
    const = lambda i: (0, 0)
    prompt_map = lambda i: (jnp.minimum(i, prompt_tiles - 1), 0)
    hbm = pl.BlockSpec(memory_space=pl.ANY)
    return pl.pallas_call(
        functools.partial(_ffn_kernel, layer=layer, final_norm=final_norm, prompt_tiles=prompt_tiles),
        grid=(prompt_tiles + 1,),
        in_specs=[
            pl.BlockSpec((FFN_ROWS, D_MODEL), prompt_map),
            pl.BlockSpec((FFN_ROWS, D_MODEL), const),
            pl.BlockSpec((1, D_MODEL), const),
            hbm, hbm, hbm,
            pl.BlockSpec((1, D_MODEL), const),
        ],
        out_specs=[pl.BlockSpec((FFN_ROWS, D_MODEL), prompt_map), pl.BlockSpec((FFN_ROWS, D_MODEL), const)],
        out_shape=[jax.ShapeDtypeStruct(xp.shape, F32), jax.ShapeDtypeStruct(xs.shape, F32)],
        scratch_shapes=[
            pltpu.VMEM((D_MODEL, D_FF), BF16), pltpu.VMEM((D_MODEL, D_FF), BF16), pltpu.VMEM((D_FF, D_MODEL), BF16),
            pltpu.VMEM((STAGE_SLOTS, STAGE_IN_ROWS, D_FF), F32),
            pltpu.VMEM((STAGE_SLOTS, STAGE_OUT_ROWS, D_MODEL), F32),
            pltpu.SemaphoreType.DMA((2, STAGE_SLOTS)),
        ],
        compiler_params=pltpu.CompilerParams(
            dimension_semantics=("arbitrary",), vmem_limit_bytes=VMEM_LIMIT),
        name="ffn",
    )(xp, xs, g, wg, wu, wd, gf)


def _log_sigmoid(x):
    return jnp.minimum(x, 0.0) - jnp.log1p(jnp.exp(-jnp.abs(x)))


def _s5_post(y, u, d, wglu, g):
    z = jax.nn.gelu(y + d * u)
    out = z * jax.nn.sigmoid(_dot(z.astype(BF16), wglu))
    return _rms(out, g)


def _lru_gates(xc, wax, b_a, b_x, lam):
    ga = _dot(xc.astype(BF16), wax)
    r = jax.nn.sigmoid(ga[:, :LRU_WIDTH] + b_a)
    i = jax.nn.sigmoid(ga[:, LRU_WIDTH:] + b_x)
    log_a = LRU_C * r * _log_sigmoid(lam)
    a = jnp.exp(log_a)
    t = jnp.tanh(log_a)
    mult = jnp.sqrt(-2.0 * t / (1.0 - t))
    return a, mult * i * xc


def _sink_softmax(s, sink_col):
    m = jnp.maximum(jnp.max(s, axis=-1, keepdims=True), sink_col)
    e = jnp.exp(s - m)
    den = jnp.sum(e, axis=-1, keepdims=True) + jnp.exp(sink_col - m)
    return e / den


def _attend(q_rows, kk, vv, sink_ref, nq, valid):
    outs = []
    for kvh in range(N_KV):
        qs = jnp.concatenate(
            [q_rows[:, (kvh * Q_PER_KV + g) * HEAD_DIM:(kvh * Q_PER_KV + g + 1) * HEAD_DIM]
             for g in range(Q_PER_KV)], axis=0)
        kh = kk[:, kvh * HEAD_DIM:(kvh + 1) * HEAD_DIM]
        vh = vv[:, kvh * HEAD_DIM:(kvh + 1) * HEAD_DIM]
        s = lax.dot_general(qs, kh, (((1,), (1,)), ((), ())), preferred_element_type=F32) * ATT_SCALE
        if valid is not None:
            s = jnp.where(valid, s, NEG_INF)
        sink_col = jnp.concatenate(
            [jnp.full((nq, 1), sink_ref[kvh * Q_PER_KV + g], F32) for g in range(Q_PER_KV)], axis=0)
        p = _sink_softmax(s, sink_col).astype(BF16)
        o = _dot(p, vh)
        outs.extend(o[g * nq:(g + 1) * nq, :] for g in range(Q_PER_KV))
    return jnp.concatenate(outs, axis=1)


def _mix_out(x, ys, ya, yl, gout, wout):
    ycat = jnp.concatenate([
        ys,
        _rms(ya, gout[:, SSM_WIDTH:SSM_WIDTH + ATT_WIDTH]),
        _rms(yl, gout[:, SSM_WIDTH + ATT_WIDTH:]),
    ], axis=1).astype(BF16)
    return x + _dot(ycat, wout)


def _mixer_prompt_kernel(
        x_ref, gmix_ref, win_ref, lbre_ref, lbim_ref, bbig_ref, cbig_ref, d_ref, wglu_ref, sink_ref,
        convw_ref, convb_ref, wax_ref, ba_ref, bx_ref, lam_ref, gout_ref, wout_ref, unperm_ref,
        o_ref, kout_ref, vout_ref, sre_ref, sim_ref, convout_ref, lruout_ref,
        st_re, st_im, st_lru, u_buf, up_buf, xs_buf, hs_buf, xl_buf, a_buf, b_buf, hl_buf, lb_buf,
        qz_buf, k_buf, vt_buf, yt_buf):
    j = pl.program_id(1)
    last = pl.num_programs(1) - 1
    T = MIX_ROWS

    @pl.when(j == 0)
    def _():
        st_re[...] = jnp.zeros_like(st_re)
        st_im[...] = jnp.zeros_like(st_im)
        st_lru[...] = jnp.zeros_like(st_lru)
        xl_buf[0:SUBLANES, :] = jnp.zeros((SUBLANES, LRU_WIDTH), F32)
        k_buf[0:WINDOW, :] = jnp.zeros((WINDOW, KV_WIDTH), BF16)
        vt_buf[0] = jnp.zeros((KV_WIDTH, WINDOW), BF16)

    x = x_ref[0]
    h = _rms(x, gmix_ref[...]).astype(BF16)
    proj = _dot(h, win_ref[...])
    u = proj[:, :OFF_Q]
    k = proj[:, OFF_K:OFF_V]
    v = proj[:, OFF_V:OFF_LX]
    xl = proj[:, OFF_LX:OFF_LG]
    gl = proj[:, OFF_LG:]
    gout = gout_ref[...]

    _put(u_buf, u)
    for i in range(SEG_LEN):
        up_buf[i * SEGS:(i + 1) * SEGS, :] = _get_strided(u_buf, i, SEGS, SEG_LEN)
    up = up_buf[...]
    xs_buf[...] = _dot(up.astype(BF16), bbig_ref[...])
    lr = lbre_ref[...]
    li = lbim_ref[...]
    lb_buf[0] = jnp.broadcast_to(lr, (SEGS, SSM_FLAT))
    lb_buf[1] = jnp.broadcast_to(li, (SEGS, SSM_FLAT))

    def s5_two_steps(ii, carry, store):
        hr, hi = carry
        r0 = pl.multiple_of(ii * 2 * SEGS, 2 * SEGS)
        lrb = lb_buf[0]
        lib = lb_buf[1]
        new_r, new_i = [], []
        for half in range(2):
            xr = xs_buf[pl.ds(r0 + half * SEGS, SEGS), 0:SSM_FLAT]
            xi = xs_buf[pl.ds(r0 + half * SEGS, SEGS), SSM_FLAT:2 * SSM_FLAT]
            hr, hi = lrb * hr - lib * hi + xr, lrb * hi + lib * hr + xi
            new_r.append(hr)
            new_i.append(hi)
        if store:
            hs_buf[pl.ds(r0, 2 * SEGS), 0:SSM_FLAT] = jnp.concatenate(new_r, axis=0).astype(BF16)
            hs_buf[pl.ds(r0, 2 * SEGS), SSM_FLAT:2 * SSM_FLAT] = jnp.concatenate(new_i, axis=0).astype(BF16)
        return hr, hi

    zero_seg = jnp.zeros((SEGS, SSM_FLAT), F32)
    loc_r, loc_i = lax.fori_loop(0, SEG_LEN // 2, functools.partial(s5_two_steps, store=False),
                                 (zero_seg, zero_seg))
    pr, pi = lr, li
    for _ in range(int(math.log2(SEG_LEN))):
        pr, pi = _cmul(pr, pi, pr, pi)
    gr, gi = st_re[...], st_im[...]
    rows_r, rows_i = [], []
    for s in range(SEGS):
        rows_r.append(gr)
        rows_i.append(gi)
        ar, ai = _cmul(pr, pi, gr, gi)
        gr = ar + loc_r[s:s + 1, :]
        gi = ai + loc_i[s:s + 1, :]
    st_re[...] = gr
    st_im[...] = gi
    lax.fori_loop(0, SEG_LEN // 2, functools.partial(s5_two_steps, store=True),
                  (jnp.concatenate(rows_r, axis=0), jnp.concatenate(rows_i, axis=0)))
    y = _dot(hs_buf[...], cbig_ref[...])
    ys_regrouped = _s5_post(y, up, d_ref[...], wglu_ref[...], gout[:, :SSM_WIDTH]).astype(BF16)
    ys = _dot(unperm_ref[...], ys_regrouped)

    xl_buf[SUBLANES:SUBLANES + T, :] = xl
    cw = convw_ref[...]
    xc = convb_ref[...] + cw[0:1, :] * xl_buf[SUBLANES - 3:SUBLANES - 3 + T, :]
    for t in range(1, CONV_WIDTH):
        xc = xc + cw[t:t + 1, :] * xl_buf[SUBLANES - 3 + t:SUBLANES - 3 + t + T, :]
    tail = xl_buf[T:T + SUBLANES, :]
    xl_buf[0:SUBLANES, :] = tail
    a, b = _lru_gates(xc, wax_ref[...], ba_ref[...], bx_ref[...], lam_ref[...])
    a3 = a.reshape(T // SUBLANES, SUBLANES, LRU_WIDTH)
    b3 = b.reshape(T // SUBLANES, SUBLANES, LRU_WIDTH)
    slab_row = lax.broadcasted_iota(jnp.int32, a3.shape, 1)
    shift = 1
    while shift < SUBLANES:
        keep = slab_row >= shift
        b3 = jnp.where(keep, a3 * pltpu.roll(b3, shift, axis=1) + b3, b3)
        a3 = jnp.where(keep, a3 * pltpu.roll(a3, shift, axis=1), a3)
        shift *= 2
    a_buf[...] = a3.reshape(T, LRU_WIDTH)
    b_buf[...] = b3.reshape(T, LRU_WIDTH)

    def lru_slab(i, carry):
        r0 = pl.multiple_of(i * SUBLANES, SUBLANES)
        hh = a_buf[pl.ds(r0, SUBLANES), :] * carry + b_buf[pl.ds(r0, SUBLANES), :]
        hl_buf[pl.ds(r0, SUBLANES), :] = hh
        return jnp.broadcast_to(hh[SUBLANES - 1:SUBLANES, :], (SUBLANES, LRU_WIDTH))

    lru_carry = lax.fori_loop(0, T // SUBLANES, lru_slab,
                              jnp.broadcast_to(st_lru[...], (SUBLANES, LRU_WIDTH)), unroll=4)
    gl_state = lru_carry[0:1, :]
    st_lru[...] = gl_state
    yl = jax.nn.gelu(gl) * hl_buf[...]

    lo = lax.broadcasted_iota(jnp.int32, (T, LANES), 1) < HEAD_DIM
    for jp in range(N_HEADS // 2):
        pair = proj[:, OFF_Q + jp * LANES:OFF_Q + (jp + 1) * LANES] * ATT_SCALE
        swapped = pltpu.roll(pair, HEAD_DIM, axis=1)
        if jp < N_HEADS // 4:
            even, odd = jnp.where(lo, pair, 0.0), jnp.where(lo, swapped, 0.0)
        else:
            even, odd = jnp.where(lo, 0.0, swapped), jnp.where(lo, 0.0, pair)
        qz_buf[2 * jp] = even.astype(BF16)
        qz_buf[2 * jp + 1] = odd.astype(BF16)
    k_buf[WINDOW:WINDOW + T, :] = k.astype(BF16)
    vt = v.T.astype(BF16)
    for i in range(T // WINDOW):
        vt_buf[i + 1] = vt[:, i * WINDOW:(i + 1) * WINDOW]

    span = 2 * CHUNK
    group = Q_PER_KV * span
    krow = lax.broadcasted_iota(jnp.int32, (2 * span, group), 0)
    qcol = lax.broadcasted_iota(jnp.int32, (2 * span, group), 1) % span
    band = jnp.logical_or(jnp.logical_and(qcol < CHUNK, krow < WINDOW + CHUNK),
                          jnp.logical_and(qcol >= CHUNK, krow >= CHUNK))
    for m in range(T // span):
        valid = band if m > 0 else jnp.logical_and(band, jnp.logical_or(krow >= WINDOW, j > 0))
        top, bot = (CHUNK if m > 0 else WINDOW), WINDOW + CHUNK
        kwin = k_buf[m * span:(m + 2) * span, :]
        vtwin = jnp.concatenate([vt_buf[m], vt_buf[m + 1]], axis=1)
        for kvh in range(N_KV):
            heads = range(kvh * Q_PER_KV, (kvh + 1) * Q_PER_KV)
            qz = jnp.concatenate([qz_buf[h, m * span:(m + 1) * span, :] for h in heads], axis=0)
            s = lax.dot_general(kwin, qz, (((1,), (1,)), ((), ())), preferred_element_type=F32)
            s = jnp.concatenate([jnp.where(valid[:top], s[:top], NEG_INF), s[top:bot],
                                 jnp.where(valid[bot:], s[bot:], NEG_INF)], axis=0)
            sink_row = jnp.concatenate([jnp.full((1, span), sink_ref[h], F32) for h in heads], axis=1)
            mx = jnp.maximum(jnp.max(s, axis=0, keepdims=True), sink_row)
            e = jnp.exp(s - mx)
            den = jnp.sum(e, axis=0, keepdims=True) + jnp.exp(sink_row - mx)
            o = _dot(vtwin[kvh * HEAD_DIM:(kvh + 1) * HEAD_DIM, :], e.astype(BF16)) * (1.0 / den)
            for g, h in enumerate(heads):
                yt_buf[h * HEAD_DIM:(h + 1) * HEAD_DIM, m * span:(m + 1) * span] = o[:, g * span:(g + 1) * span]
    k_buf[0:WINDOW, :] = k_buf[T:T + WINDOW, :]
    vt_buf[0] = vt_buf[T // WINDOW]

    o_ref[0] = _mix_out(x, ys, yt_buf[...].T, yl, gout, wout_ref[...])

    @pl.when(j == last)
    def _():
        kout_ref[0] = k[T - WINDOW:, :]
        vout_ref[0] = v[T - WINDOW:, :]
        sre_ref[0] = gr
        sim_ref[0] = gi
        convout_ref[0] = tail[SUBLANES - (CONV_WIDTH - 1):, :]
        lruout_ref[0] = gl_state


def _mixer_prompt(x, p):
    bn, L, _ = x.shape
    T = MIX_ROWS
    const2 = lambda b, j: (0, 0)
    vspec = lambda shape: pl.BlockSpec(shape, const2)
    per_b = lambda shape: pl.BlockSpec((1,) + shape, lambda b, j: (b, 0, 0))
    in_specs = [
        pl.BlockSpec((1, T, D_MODEL), lambda b, j: (b, j, 0)),
        vspec((1, D_MODEL)), vspec((D_MODEL, IN_WIDTH)),
        vspec((1, SSM_FLAT)), vspec((1, SSM_FLAT)),
        vspec((SSM_WIDTH, 2 * SSM_FLAT)), vspec((2 * SSM_FLAT, SSM_WIDTH)),
        vspec((1, SSM_WIDTH)), vspec((SSM_WIDTH, SSM_WIDTH)),
        pl.BlockSpec(memory_space=pltpu.SMEM),
        vspec((CONV_WIDTH, LRU_WIDTH)), vspec((1, LRU_WIDTH)),
        vspec((LRU_WIDTH, 2 * LRU_WIDTH)), vspec((1, LRU_WIDTH)), vspec((1, LRU_WIDTH)), vspec((1, LRU_WIDTH)),
        vspec((1, D_MODEL)), vspec((D_MODEL, D_MODEL)), vspec((T, T)),
    ]
    out_specs = [
        pl.BlockSpec((1, T, D_MODEL), lambda b, j: (b, j, 0)),
        per_b((WINDOW, KV_WIDTH)), per_b((WINDOW, KV_WIDTH)),
        per_b((1, SSM_FLAT)), per_b((1, SSM_FLAT)),
        per_b((CONV_WIDTH - 1, LRU_WIDTH)), per_b((1, LRU_WIDTH)),
    ]
    out_shape = [
        jax.ShapeDtypeStruct((bn, L, D_MODEL), F32),
        jax.ShapeDtypeStruct((bn, WINDOW, KV_WIDTH), F32), jax.ShapeDtypeStruct((bn, WINDOW, KV_WIDTH), F32),
        jax.ShapeDtypeStruct((bn, 1, SSM_FLAT), F32), jax.ShapeDtypeStruct((bn, 1, SSM_FLAT), F32),
        jax.ShapeDtypeStruct((bn, CONV_WIDTH - 1, LRU_WIDTH), F32), jax.ShapeDtypeStruct((bn, 1, LRU_WIDTH), F32),
    ]
    planes = lambda width: pltpu.VMEM((width // LANES, T, LANES), F32)
    scratch = [
        pltpu.VMEM((1, SSM_FLAT), F32), pltpu.VMEM((1, SSM_FLAT), F32), pltpu.VMEM((1, LRU_WIDTH), F32),
        planes(SSM_WIDTH), pltpu.VMEM((T, SSM_WIDTH), F32),
        pltpu.VMEM((T, 2 * SSM_FLAT), F32), pltpu.VMEM((T, 2 * SSM_FLAT), BF16),
        pltpu.VMEM((T + SUBLANES, LRU_WIDTH), F32),
        pltpu.VMEM((T, LRU_WIDTH), F32), pltpu.VMEM((T, LRU_WIDTH), F32), pltpu.VMEM((T, LRU_WIDTH), F32),
        pltpu.VMEM((2, SEGS, SSM_FLAT), F32),
        pltpu.VMEM((N_HEADS, T, LANES), BF16),
        pltpu.VMEM((T + WINDOW, KV_WIDTH), BF16),
        pltpu.VMEM((T // WINDOW + 1, KV_WIDTH, WINDOW), BF16),
        pltpu.VMEM((ATT_WIDTH, T), F32),
    ]
    return pl.pallas_call(
        _mixer_prompt_kernel,
        grid=(bn, L // T),
        in_specs=in_specs, out_specs=out_specs, out_shape=out_shape, scratch_shapes=scratch,
        compiler_params=pltpu.CompilerParams(
            dimension_semantics=("arbitrary", "arbitrary"), vmem_limit_bytes=VMEM_LIMIT),
        name="mixer_prompt",
    )(x, p["gmix"], p["win"], p["lb_re"], p["lb_im"], p["bbig"], p["cbig"], p["d"], p["wglu"], p["sink"],
      p["convw"], p["convb"], p["wax"], p["b_a"], p["b_x"], p["lam"], p["gout"], p["wout"], _unpermute_matrix())


def _unpermute_matrix():
    token = jnp.arange(MIX_ROWS)
    source = (token % SEG_LEN) * SEGS + token // SEG_LEN
    return (source[:, None] == jnp.arange(MIX_ROWS)[None, :]).astype(BF16)


def _mixer_sample_kernel(
        x_ref, ck_ref, cv_ref, s0re_ref, s0im_ref, conv0_ref, lru0_ref,
        gmix_ref, win_ref, lbre_ref, lbim_ref, bbig_ref, cbig_ref, d_ref, wglu_ref, sink_ref,
        convw_ref, convb_ref, wax_ref, ba_ref, bx_ref, lam_ref, gout_ref, wout_ref,
        o_ref, kout_ref, vout_ref, sre_ref, sim_ref, convout_ref, lruout_ref,
        xs_buf, xp_buf, a_buf, b_buf, hl_buf, q_buf, ya_buf, *, nb, ns):
    rows = nb * ns
    x = x_ref[...]
    h = _rms(x, gmix_ref[...]).astype(BF16)
    proj = _dot(h, win_ref[...])
    u = proj[:, :OFF_Q]
    k = proj[:, OFF_K:OFF_V]
    v = proj[:, OFF_V:OFF_LX]
    xl = proj[:, OFF_LX:OFF_LG]
    gl = proj[:, OFF_LG:]
    gout = gout_ref[...]
    kout_ref[...] = k
    vout_ref[...] = v

    xs_buf[...] = _dot(u.astype(BF16), bbig_ref[...])
    lr = lbre_ref[...]
    li = lbim_ref[...]
    hr = s0re_ref[...]
    hi = s0im_ref[...]
    for t in range(ns):
        sl = slice(t * nb, (t + 1) * nb)
        nr = lr * hr - li * hi + xs_buf[sl, 0:SSM_FLAT]
        ni = lr * hi + li * hr + xs_buf[sl, SSM_FLAT:2 * SSM_FLAT]
        hr, hi = nr, ni
        xs_buf[sl, 0:SSM_FLAT] = hr
        xs_buf[sl, SSM_FLAT:2 * SSM_FLAT] = hi
    sre_ref[...] = hr
    sim_ref[...] = hi
    y = _dot(xs_buf[...].astype(BF16), cbig_ref[...])
    ys = _s5_post(y, u, d_ref[...], wglu_ref[...], gout[:, :SSM_WIDTH])

    npre = (CONV_WIDTH - 1) * nb
    xp_buf[0:npre, :] = conv0_ref[...]
    xp_buf[npre:npre + rows, :] = xl
    cw = convw_ref[...]
    xc = convb_ref[...] + cw[0:1, :] * xp_buf[0:rows, :]
    for t in range(1, CONV_WIDTH):
        xc = xc + cw[t:t + 1, :] * xp_buf[t * nb:t * nb + rows, :]
    convout_ref[...] = xp_buf[rows:rows + npre, :]
    a, b = _lru_gates(xc, wax_ref[...], ba_ref[...], bx_ref[...], lam_ref[...])
    a_buf[...] = a
    b_buf[...] = b
    hh = lru0_ref[...]
    for t in range(ns):
        sl = slice(t * nb, (t + 1) * nb)
        hh = a_buf[sl, :] * hh + b_buf[sl, :]
        hl_buf[sl, :] = hh
    lruout_ref[...] = hh
    yl = jax.nn.gelu(gl) * hl_buf[...]

    _put(q_buf, proj[:, OFF_Q:OFF_K])

    def attn_stream(bi, _):
        qb = _get_strided(q_buf, bi, ns, nb).astype(BF16)
        kn = kout_ref[pl.ds(bi, ns, stride=nb), :]
        vn = vout_ref[pl.ds(bi, ns, stride=nb), :]
        kk = jnp.concatenate([ck_ref[bi], kn], axis=0).astype(BF16)
        vv = jnp.concatenate([cv_ref[bi], vn], axis=0).astype(BF16)
        _put_strided(ya_buf, bi, ns, nb, _attend(qb, kk, vv, sink_ref, ns, None))
        return 0

    lax.fori_loop(0, nb, attn_stream, 0)
    o_ref[...] = _mix_out(x, ys, _get(ya_buf), yl, gout, wout_ref[...])


def _mixer_sample(x, ck, cv, s0re, s0im, conv0, lru0, p, nb, ns):
    rows = nb * ns
    vm = pl.BlockSpec(memory_space=pltpu.VMEM)
    in_specs = [vm] * 15 + [pl.BlockSpec(memory_space=pltpu.SMEM)] + [vm] * 8
    out_shape = [
        jax.ShapeDtypeStruct((rows, D_MODEL), F32),
        jax.ShapeDtypeStruct((rows, KV_WIDTH), F32), jax.ShapeDtypeStruct((rows, KV_WIDTH), F32),
        jax.ShapeDtypeStruct((nb, SSM_FLAT), F32), jax.ShapeDtypeStruct((nb, SSM_FLAT), F32),
        jax.ShapeDtypeStruct(((CONV_WIDTH - 1) * nb, LRU_WIDTH), F32), jax.ShapeDtypeStruct((nb, LRU_WIDTH), F32),
    ]
    scratch = [
        pltpu.VMEM((rows, 2 * SSM_FLAT), F32),
        pltpu.VMEM((rows + (CONV_WIDTH - 1) * nb, LRU_WIDTH), F32),
        pltpu.VMEM((rows, LRU_WIDTH), F32), pltpu.VMEM((rows, LRU_WIDTH), F32), pltpu.VMEM((rows, LRU_WIDTH), F32),
        pltpu.VMEM((ATT_WIDTH // LANES, rows, LANES), F32),
        pltpu.VMEM((ATT_WIDTH // LANES, rows, LANES), F32),
    ]
    return pl.pallas_call(
        functools.partial(_mixer_sample_kernel, nb=nb, ns=ns),
        in_specs=in_specs, out_specs=[vm] * 7, out_shape=out_shape, scratch_shapes=scratch,
        compiler_params=pltpu.CompilerParams(vmem_limit_bytes=VMEM_LIMIT),
        name="mixer_sample",
    )(x, ck, cv, s0re, s0im, conv0, lru0,
      p["gmix"], p["win"], p["lb_re"], p["lb_im"], p["bbig"], p["cbig"], p["d"], p["wglu"], p["sink"],
      p["convw"], p["convb"], p["wax"], p["b_a"], p["b_x"], p["lam"], p["gout"], p["wout"])


def _block_diag(w):
    n, i, o = w.shape
    eye = jnp.eye(n, dtype=w.dtype)
    return jnp.einsum("nm,nio->nimo", eye, w).reshape(n * i, n * o)


def _layer_params(l, mix_norm, w_in, ssm_a_re, ssm_a_im, ssm_log_dt, ssm_b_re, ssm_b_im, ssm_c_re, ssm_c_im,
                  ssm_d, ssm_w_glu, attn_sink, conv_w, conv_b, lru_w_a, lru_b_a, lru_w_x, lru_b_x, lru_lambda,
                  out_norm, w_out):
    a_re = ssm_a_re[l]
    a_im = ssm_a_im[l]
    dt = jnp.exp(ssm_log_dt[l])[:, None]
    mag = jnp.exp(a_re * dt)
    lb_re = mag * jnp.cos(a_im * dt)
    lb_im = mag * jnp.sin(a_im * dt)
    den = a_re * a_re + a_im * a_im
    nr = lb_re - 1.0
    k_re = (nr * a_re + lb_im * a_im) / den
    k_im = (lb_im * a_re - nr * a_im) / den
    b_re = ssm_b_re[l]
    b_im = ssm_b_im[l]
    bb_re = k_re[..., None] * b_re - k_im[..., None] * b_im
    bb_im = k_re[..., None] * b_im + k_im[..., None] * b_re
    to_gh_p = lambda m: jnp.transpose(m, (0, 2, 1))
    bbig = jnp.concatenate([_block_diag(to_gh_p(bb_re)), _block_diag(to_gh_p(bb_im))], axis=1)
    to_gp_h = lambda m: jnp.transpose(m, (0, 2, 1))
    cbig = jnp.concatenate([_block_diag(to_gp_h(ssm_c_re[l])), -_block_diag(to_gp_h(ssm_c_im[l]))], axis=0)
    row = lambda vec: vec.reshape(1, -1)
    return {
        "gmix": row(mix_norm[l]), "win": w_in[l].astype(BF16),
        "lb_re": row(lb_re), "lb_im": row(lb_im),
        "bbig": bbig.astype(BF16), "cbig": cbig.astype(BF16),
        "d": row(ssm_d[l]), "wglu": ssm_w_glu[l].astype(BF16), "sink": attn_sink[l],
        "convw": conv_w[l], "convb": row(conv_b[l]),
        "wax": jnp.concatenate([_block_diag(lru_w_a[l]), _block_diag(lru_w_x[l])], axis=1).astype(BF16),
        "b_a": row(lru_b_a[l]), "b_x": row(lru_b_x[l]), "lam": row(lru_lambda[l]),
        "gout": row(out_norm[l]), "wout": w_out[l].astype(BF16),
    }


def kernel(x_prompt, x_sample, cache_k, cache_v, state_ssm_re, state_ssm_im, state_conv, state_lru,
           ffn1_norm, ffn1_w_gate, ffn1_w_up, ffn1_w_down, mix_norm, w_in,
           ssm_a_re, ssm_a_im, ssm_log_dt, ssm_b_re, ssm_b_im, ssm_c_re, ssm_c_im, ssm_d, ssm_w_glu,
           attn_sink, conv_w, conv_b, lru_w_a, lru_b_a, lru_w_x, lru_b_x, lru_lambda,
           out_norm, w_out, ffn2_norm, ffn2_w_gate, ffn2_w_up, ffn2_w_down, final_norm):
    bn, L, _ = x_prompt.shape
    nb, ns, _ = x_sample.shape
    row = lambda vec: vec.reshape(1, -1)
    gfin = row(final_norm)

    xp = x_prompt.reshape(bn * L, D_MODEL)
    xs = jnp.transpose(x_sample, (1, 0, 2)).reshape(ns * nb, D_MODEL)
    prompt_states = [[] for _ in range(6)]
    sample_states = [[] for _ in range(6)]
    for l in range(DEPTH):
        p = _layer_params(l, mix_norm, w_in, ssm_a_re, ssm_a_im, ssm_log_dt, ssm_b_re, ssm_b_im, ssm_c_re,
                          ssm_c_im, ssm_d, ssm_w_glu, attn_sink, conv_w, conv_b, lru_w_a, lru_b_a, lru_w_x,
                          lru_b_x, lru_lambda, out_norm, w_out)
        f1 = (l, row(ffn1_norm[l]), ffn1_w_gate, ffn1_w_up, ffn1_w_down)
        f2 = (l, row(ffn2_norm[l]), ffn2_w_gate, ffn2_w_up, ffn2_w_down)
        is_last = l == DEPTH - 1

        xp, xs = _ffn(xp, xs, *f1, gfin, False)
        xp3, kp, vp, srp, sip, cvp, lrp = _mixer_prompt(xp.reshape(bn, L, D_MODEL), p)
        xp = xp3.reshape(bn * L, D_MODEL)
        for lst, s in zip(prompt_states, (
                kp.reshape(bn, WINDOW, N_KV, HEAD_DIM), vp.reshape(bn, WINDOW, N_KV, HEAD_DIM),
                srp.reshape(bn, SSM_GROUPS, SSM_STATE), sip.reshape(bn, SSM_GROUPS, SSM_STATE),
                cvp, lrp.reshape(bn, LRU_WIDTH))):
            lst.append(s)

        ck = cache_k[l].reshape(nb, -1, KV_WIDTH)
        cv = cache_v[l].reshape(nb, -1, KV_WIDTH)
        conv0 = jnp.transpose(state_conv[l], (1, 0, 2)).reshape((CONV_WIDTH - 1) * nb, LRU_WIDTH)
        xs, ks, vs, srs, sis, cvs, lrs = _mixer_sample(
            xs, ck, cv, state_ssm_re[l].reshape(nb, SSM_FLAT), state_ssm_im[l].reshape(nb, SSM_FLAT),
            conv0, state_lru[l], p, nb, ns)
        xp, xs = _ffn(xp, xs, *f2, gfin, is_last)
        unflip = lambda m, w: jnp.transpose(m.reshape(-1, nb, w), (1, 0, 2))
        for lst, s in zip(sample_states, (
                unflip(ks, KV_WIDTH).reshape(nb, ns, N_KV, HEAD_DIM),
                unflip(vs, KV_WIDTH).reshape(nb, ns, N_KV, HEAD_DIM),
                srs.reshape(nb, SSM_GROUPS, SSM_STATE), sis.reshape(nb, SSM_GROUPS, SSM_STATE),
                unflip(cvs, LRU_WIDTH), lrs)):
            lst.append(s)

    y_prompt = xp.reshape(bn, L, D_MODEL)
    y_sample = jnp.transpose(xs.reshape(ns, nb, D_MODEL), (1, 0, 2))
    return (y_prompt, y_sample, *[jnp.stack(c, axis=0) for c in prompt_states],
            *[jnp.stack(c, axis=0) for c in sample_states])
```

```python
import functools
import math

import jax
import jax.numpy as jnp
from jax import lax
from jax.experimental import pallas as pl
from jax.experimental.pallas import tpu as pltpu

F32 = jnp.float32
BF16 = jnp.bfloat16

D_MODEL = 1024
D_FF = 2816
DEPTH = 2
CHUNK = 64
SSM_WIDTH = 256
SSM_GROUP = 16
SSM_GROUPS = 16
SSM_STATE = 64
SSM_FLAT = SSM_GROUPS * SSM_STATE
HEAD_DIM = 64
ATT_WIDTH = 512
N_HEADS = 8
N_KV = 2
Q_PER_KV = 4
KV_WIDTH = 128
WINDOW = 128
LRU_WIDTH = 256
LRU_BLOCKS = 4
LRU_BLOCK = 64
CONV_WIDTH = 4
LRU_C = 8.0
OFF_Q = SSM_WIDTH
OFF_K = OFF_Q + ATT_WIDTH
OFF_V = OFF_K + KV_WIDTH
OFF_LX = OFF_V + KV_WIDTH
OFF_LG = OFF_LX + LRU_WIDTH
IN_WIDTH = OFF_LG + LRU_WIDTH
ATT_SCALE = HEAD_DIM ** -0.5
assert math.frexp(ATT_SCALE)[0] == 0.5, "the prompt mixer pre-scales q, exact only for a power-of-two scale"
EPS = 1e-6
NEG_INF = -1e30

SUBLANES = 8
LANES = 128
FFN_ROWS = 512
FFN_COLS = 256
STAGE_SLOTS = 4
STAGE_IN_ROWS = 128
STAGE_OUT_ROWS = 256
MIX_ROWS = 512
SEGS = SUBLANES
SEG_LEN = MIX_ROWS // SEGS
VMEM_LIMIT = 56 * 1024 * 1024


def _rms(x, g):
    return x * lax.rsqrt(jnp.mean(x * x, axis=-1, keepdims=True) + EPS) * g


def _dot(a, b):
    return jnp.dot(a, b, preferred_element_type=F32)


def _cmul(ar, ai, br, bi):
    return ar * br - ai * bi, ar * bi + ai * br


def _put(ref, val):
    for c in range(ref.shape[0]):
        ref[c] = val[:, c * LANES:(c + 1) * LANES]


def _get(ref):
    return jnp.concatenate([ref[c] for c in range(ref.shape[0])], axis=1)


def _get_strided(ref, start, n, stride):
    return jnp.concatenate(
        [ref[c, pl.ds(start, n, stride=stride), :] for c in range(ref.shape[0])], axis=1)


def _put_strided(ref, start, n, stride, val):
    for c in range(ref.shape[0]):
        ref[c, pl.ds(start, n, stride=stride), :] = val[:, c * LANES:(c + 1) * LANES]


def _weight_jobs(layer, wg_hbm, wu_hbm, wd_hbm, wg_v, wu_v, wd_v, stage_in, stage_out, sem):
    jobs = []
    used = [0, 0]

    def add(src, dst, stage, kind, r0, n):
        slot = used[kind] % STAGE_SLOTS
        used[kind] += 1
        copy = pltpu.make_async_copy(src.at[layer, pl.ds(r0, n), :], stage.at[slot], sem.at[kind, slot])
        jobs.append((copy, stage.at[slot], dst, r0, n))

    for r0 in range(0, D_MODEL, STAGE_IN_ROWS):
        add(wg_hbm, wg_v, stage_in, 0, r0, STAGE_IN_ROWS)
        add(wu_hbm, wu_v, stage_in, 0, r0, STAGE_IN_ROWS)
    for r0 in range(0, D_FF, STAGE_OUT_ROWS):
        add(wd_hbm, wd_v, stage_out, 1, r0, STAGE_OUT_ROWS)
    return jobs


def _ffn_kernel(xp_ref, xs_ref, g_ref, wg_hbm, wu_hbm, wd_hbm, gf_ref, op_ref, os_ref,
                wg_ref, wu_ref, wd_ref, stage_in, stage_out, sem, *, layer, final_norm, prompt_tiles):
    i = pl.program_id(0)

    @pl.when(i == 0)
    def _():
        jobs = _weight_jobs(layer, wg_hbm, wu_hbm, wd_hbm, wg_ref, wu_ref, wd_ref, stage_in, stage_out, sem)
        ahead = STAGE_SLOTS - 1
        for k in range(min(ahead, len(jobs))):
            jobs[k][0].start()
        for k, (copy, staged, dst, r0, n) in enumerate(jobs):
            if k + ahead < len(jobs):
                jobs[k + ahead][0].start()
            copy.wait()
            dst[r0:r0 + n, :] = staged[...].astype(BF16)

    x = jnp.where(i < prompt_tiles, xp_ref[...], xs_ref[...])
    h = _rms(x, g_ref[...]).astype(BF16)
    acc = jnp.zeros(x.shape, F32)
    for c in range(D_FF // FFN_COLS):
        sl = slice(c * FFN_COLS, (c + 1) * FFN_COLS)
        g = _dot(h, wg_ref[:, sl])
        u = _dot(h, wu_ref[:, sl])
        a = (g * jax.nn.sigmoid(g) * u).astype(BF16)
        acc = acc + _dot(a, wd_ref[sl, :])
    y = x + 0.5 * acc
    if final_norm:
        y = _rms(y, gf_ref[...])

    @pl.when(i < prompt_tiles)
    def _():
        op_ref[...] = y

    @pl.when(i >= prompt_tiles)
    def _():
        os_ref[...] = y


def _ffn(xp, xs, layer, g, wg, wu, wd, gf, final_norm):
    prompt_tiles = xp.shape[0] // FFN_ROWS
    assert xs.shape[0] == FFN_ROWS
    const = lambda i: (0, 0)
    prompt_map = lambda i: (jnp.minimum(i, prompt_tiles - 1), 0)
    hbm = pl.BlockSpec(memory_space=pl.ANY)
    return pl.pallas_call(
        functools.partial(_ffn_kernel, layer=layer, final_norm=final_norm, prompt_tiles=prompt_tiles),
        grid=(prompt_tiles + 1,),
        in_specs=[
            pl.BlockSpec((FFN_ROWS, D_MODEL), prompt_map),
            pl.BlockSpec((FFN_ROWS, D_MODEL), const),
            pl.BlockSpec((1, D_MODEL), const),
            hbm, hbm, hbm,
            pl.BlockSpec((1, D_MODEL), const),
        ],
        out_specs=[pl.BlockSpec((FFN_ROWS, D_MODEL), prompt_map), pl.BlockSpec((FFN_ROWS, D_MODEL), const)],
        out_shape=[jax.ShapeDtypeStruct(xp.shape, F32), jax.ShapeDtypeStruct(xs.shape, F32)],
        scratch_shapes=[
            pltpu.VMEM((D_MODEL, D_FF), BF16), pltpu.VMEM((D_MODEL, D_FF), BF16), pltpu.VMEM((D_FF, D_MODEL), BF16),
            pltpu.VMEM((STAGE_SLOTS, STAGE_IN_ROWS, D_FF), F32),
            pltpu.VMEM((STAGE_SLOTS, STAGE_OUT_ROWS, D_MODEL), F32),
            pltpu.SemaphoreType.DMA((2, STAGE_SLOTS)),
        ],
        compiler_params=pltpu.CompilerParams(
            dimension_semantics=("arbitrary",), vmem_limit_bytes=VMEM_LIMIT),
        name="ffn",
    )(xp, xs, g, wg, wu, wd, gf)


def _log_sigmoid(x):
    return jnp.minimum(x, 0.0) - jnp.log1p(jnp.exp(-jnp.abs(x)))


def _s5_post(y, u, d, wglu, g):
    z = jax.nn.gelu(y + d * u)
    out = z * jax.nn.sigmoid(_dot(z.astype(BF16), wglu))
    return _rms(out, g)


def _lru_gates(xc, wax, b_a, b_x, lam):
    ga = _dot(xc.astype(BF16), wax)
    r = jax.nn.sigmoid(ga[:, :LRU_WIDTH] + b_a)
    i = jax.nn.sigmoid(ga[:, LRU_WIDTH:] + b_x)
    log_a = LRU_C * r * _log_sigmoid(lam)
    a = jnp.exp(log_a)
    t = jnp.tanh(log_a)
    mult = jnp.sqrt(-2.0 * t / (1.0 - t))
    return a, mult * i * xc


def _sink_softmax(s, sink_col):
    m = jnp.maximum(jnp.max(s, axis=-1, keepdims=True), sink_col)
    e = jnp.exp(s - m)
    den = jnp.sum(e, axis=-1, keepdims=True) + jnp.exp(sink_col - m)
    return e / den


def _attend(q_rows, kk, vv, sink_ref, nq, valid):
    outs = []
    for kvh in range(N_KV):
        qs = jnp.concatenate(
            [q_rows[:, (kvh * Q_PER_KV + g) * HEAD_DIM:(kvh * Q_PER_KV + g + 1) * HEAD_DIM]
             for g in range(Q_PER_KV)], axis=0)
        kh = kk[:, kvh * HEAD_DIM:(kvh + 1) * HEAD_DIM]
        vh = vv[:, kvh * HEAD_DIM:(kvh + 1) * HEAD_DIM]
        s = lax.dot_general(qs, kh, (((1,), (1,)), ((), ())), preferred_element_type=F32) * ATT_SCALE
        if valid is not None:
            s = jnp.where(valid, s, NEG_INF)
        sink_col = jnp.concatenate(
            [jnp.full((nq, 1), sink_ref[kvh * Q_PER_KV + g], F32) for g in range(Q_PER_KV)], axis=0)
        p = _sink_softmax(s, sink_col).astype(BF16)
        o = _dot(p, vh)
        outs.extend(o[g * nq:(g + 1) * nq, :] for g in range(Q_PER_KV))
    return jnp.concatenate(outs, axis=1)


def _mix_out(x, ys, ya, yl, gout, wout):
    ycat = jnp.concatenate([
        ys,
        _rms(ya, gout[:, SSM_WIDTH:SSM_WIDTH + ATT_WIDTH]),
        _rms(yl, gout[:, SSM_WIDTH + ATT_WIDTH:]),
    ], axis=1).astype(BF16)
    return x + _dot(ycat, wout)


def _mixer_prompt_kernel(
        x_ref, gmix_ref, win_ref, lbre_ref, lbim_ref, bbig_ref, cbig_ref, d_ref, wglu_ref, sink_ref,
        convw_ref, convb_ref, wax_ref, ba_ref, bx_ref, lam_ref, gout_ref, wout_ref, unperm_ref,
        o_ref, kout_ref, vout_ref, sre_ref, sim_ref, convout_ref, lruout_ref,
        st_re, st_im, st_lru, u_buf, up_buf, xs_buf, hs_buf, xl_buf, a_buf, b_buf, hl_buf, lb_buf,
        qz_buf, k_buf, vt_buf, yt_buf):
    j = pl.program_id(1)
    last = pl.num_programs(1) - 1
    T = MIX_ROWS

    @pl.when(j == 0)
    def _():
        st_re[...] = jnp.zeros_like(st_re)
        st_im[...] = jnp.zeros_like(st_im)
        st_lru[...] = jnp.zeros_like(st_lru)
        xl_buf[0:SUBLANES, :] = jnp.zeros((SUBLANES, LRU_WIDTH), F32)
        k_buf[0:WINDOW, :] = jnp.zeros((WINDOW, KV_WIDTH), BF16)
        vt_buf[0] = jnp.zeros((KV_WIDTH, WINDOW), BF16)

    x = x_ref[0]
    h = _rms(x, gmix_ref[...]).astype(BF16)
    proj = _dot(h, win_ref[...])
    u = proj[:, :OFF_Q]
    k = proj[:, OFF_K:OFF_V]
    v = proj[:, OFF_V:OFF_LX]
    xl = proj[:, OFF_LX:OFF_LG]
    gl = proj[:, OFF_LG:]
    gout = gout_ref[...]

    _put(u_buf, u)
    for i in range(SEG_LEN):
        up_buf[i * SEGS:(i + 1) * SEGS, :] = _get_strided(u_buf, i, SEGS, SEG_LEN)
    up = up_buf[...]
    xs_buf[...] = _dot(up.astype(BF16), bbig_ref[...])
    lr = lbre_ref[...]
    li = lbim_ref[...]
    lb_buf[0] = jnp.broadcast_to(lr, (SEGS, SSM_FLAT))
    lb_buf[1] = jnp.broadcast_to(li, (SEGS, SSM_FLAT))

    def s5_two_steps(ii, carry, store):
        hr, hi = carry
        r0 = pl.multiple_of(ii * 2 * SEGS, 2 * SEGS)
        lrb = lb_buf[0]
        lib = lb_buf[1]
        new_r, new_i = [], []
        for half in range(2):
            xr = xs_buf[pl.ds(r0 + half * SEGS, SEGS), 0:SSM_FLAT]
            xi = xs_buf[pl.ds(r0 + half * SEGS, SEGS), SSM_FLAT:2 * SSM_FLAT]
            hr, hi = lrb * hr - lib * hi + xr, lrb * hi + lib * hr + xi
            new_r.append(hr)
            new_i.append(hi)
        if store:
            hs_buf[pl.ds(r0, 2 * SEGS), 0:SSM_FLAT] = jnp.concatenate(new_r, axis=0).astype(BF16)
            hs_buf[pl.ds(r0, 2 * SEGS), SSM_FLAT:2 * SSM_FLAT] = jnp.concatenate(new_i, axis=0).astype(BF16)
        return hr, hi

    zero_seg = jnp.zeros((SEGS, SSM_FLAT), F32)
    loc_r, loc_i = lax.fori_loop(0, SEG_LEN // 2, functools.partial(s5_two_steps, store=False),
                                 (zero_seg, zero_seg), unroll=True)
    pr, pi = lr, li
    for _ in range(int(math.log2(SEG_LEN))):
        pr, pi = _cmul(pr, pi, pr, pi)
    gr, gi = st_re[...], st_im[...]
    rows_r, rows_i = [], []
    for s in range(SEGS):
        rows_r.append(gr)
        rows_i.append(gi)
        ar, ai = _cmul(pr, pi, gr, gi)
        gr = ar + loc_r[s:s + 1, :]
        gi = ai + loc_i[s:s + 1, :]
    st_re[...] = gr
    st_im[...] = gi
    lax.fori_loop(0, SEG_LEN // 2, functools.partial(s5_two_steps, store=True),
                  (jnp.concatenate(rows_r, axis=0), jnp.concatenate(rows_i, axis=0)), unroll=True)
    y = _dot(hs_buf[...], cbig_ref[...])
    ys_regrouped = _s5_post(y, up, d_ref[...], wglu_ref[...], gout[:, :SSM_WIDTH]).astype(BF16)
    ys = _dot(unperm_ref[...], ys_regrouped)

    xl_buf[SUBLANES:SUBLANES + T, :] = xl
    cw = convw_ref[...]
    xc = convb_ref[...] + cw[0:1, :] * xl_buf[SUBLANES - 3:SUBLANES - 3 + T, :]
    for t in range(1, CONV_WIDTH):
        xc = xc + cw[t:t + 1, :] * xl_buf[SUBLANES - 3 + t:SUBLANES - 3 + t + T, :]
    tail = xl_buf[T:T + SUBLANES, :]
    xl_buf[0:SUBLANES, :] = tail
    a, b = _lru_gates(xc, wax_ref[...], ba_ref[...], bx_ref[...], lam_ref[...])
    a3 = a.reshape(T // SUBLANES, SUBLANES, LRU_WIDTH)
    b3 = b.reshape(T // SUBLANES, SUBLANES, LRU_WIDTH)
    slab_row = lax.broadcasted_iota(jnp.int32, a3.shape, 1)
    shift = 1
    while shift < SUBLANES:
        keep = slab_row >= shift
        b3 = jnp.where(keep, a3 * pltpu.roll(b3, shift, axis=1) + b3, b3)
        a3 = jnp.where(keep, a3 * pltpu.roll(a3, shift, axis=1), a3)
        shift *= 2
    a_buf[...] = a3.reshape(T, LRU_WIDTH)
    b_buf[...] = b3.reshape(T, LRU_WIDTH)

    def lru_slab(i, carry):
        r0 = pl.multiple_of(i * SUBLANES, SUBLANES)
        hh = a_buf[pl.ds(r0, SUBLANES), :] * carry + b_buf[pl.ds(r0, SUBLANES), :]
        hl_buf[pl.ds(r0, SUBLANES), :] = hh
        return jnp.broadcast_to(hh[SUBLANES - 1:SUBLANES, :], (SUBLANES, LRU_WIDTH))

    lru_carry = lax.fori_loop(0, T // SUBLANES, lru_slab,
                              jnp.broadcast_to(st_lru[...], (SUBLANES, LRU_WIDTH)), unroll=True)
    gl_state = lru_carry[0:1, :]
    st_lru[...] = gl_state
    yl = jax.nn.gelu(gl) * hl_buf[...]

    lo = lax.broadcasted_iota(jnp.int32, (T, LANES), 1) < HEAD_DIM
    for jp in range(N_HEADS // 2):
        pair = proj[:, OFF_Q + jp * LANES:OFF_Q + (jp + 1) * LANES] * ATT_SCALE
        swapped = pltpu.roll(pair, HEAD_DIM, axis=1)
        if jp < N_HEADS // 4:
            even, odd = jnp.where(lo, pair, 0.0), jnp.where(lo, swapped, 0.0)
        else:
            even, odd = jnp.where(lo, 0.0, swapped), jnp.where(lo, 0.0, pair)
        qz_buf[2 * jp] = even.astype(BF16)
        qz_buf[2 * jp + 1] = odd.astype(BF16)
    k_buf[WINDOW:WINDOW + T, :] = k.astype(BF16)
    vt = v.T.astype(BF16)
    for i in range(T // WINDOW):
        vt_buf[i + 1] = vt[:, i * WINDOW:(i + 1) * WINDOW]

    span = 2 * CHUNK
    group = Q_PER_KV * span
    krow = lax.broadcasted_iota(jnp.int32, (2 * span, group), 0)
    qcol = lax.broadcasted_iota(jnp.int32, (2 * span, group), 1) % span
    band = jnp.logical_or(jnp.logical_and(qcol < CHUNK, krow < WINDOW + CHUNK),
                          jnp.logical_and(qcol >= CHUNK, krow >= CHUNK))
    for m in range(T // span):
        valid = band if m > 0 else jnp.logical_and(band, jnp.logical_or(krow >= WINDOW, j > 0))
        top, bot = (CHUNK if m > 0 else WINDOW), WINDOW + CHUNK
        kwin = k_buf[m * span:(m + 2) * span, :]
        vtwin = jnp.concatenate([vt_buf[m], vt_buf[m + 1]], axis=1)
        for kvh in range(N_KV):
            heads = range(kvh * Q_PER_KV, (kvh + 1) * Q_PER_KV)
            qz = jnp.concatenate([qz_buf[h, m * span:(m + 1) * span, :] for h in heads], axis=0)
            s = lax.dot_general(kwin, qz, (((1,), (1,)), ((), ())), preferred_element_type=F32)
            s = jnp.concatenate([jnp.where(valid[:top], s[:top], NEG_INF), s[top:bot],
                                 jnp.where(valid[bot:], s[bot:], NEG_INF)], axis=0)
            sink_row = jnp.concatenate([jnp.full((1, span), sink_ref[h], F32) for h in heads], axis=1)
            mx = jnp.maximum(jnp.max(s, axis=0, keepdims=True), sink_row)
            e = jnp.exp(s - mx)
            den = jnp.sum(e, axis=0, keepdims=True) + jnp.exp(sink_row - mx)
            o = _dot(vtwin[kvh * HEAD_DIM:(kvh + 1) * HEAD_DIM, :], e.astype(BF16)) * (1.0 / den)
            for g, h in enumerate(heads):
                yt_buf[h * HEAD_DIM:(h + 1) * HEAD_DIM, m * span:(m + 1) * span] = o[:, g * span:(g + 1) * span]
    k_buf[0:WINDOW, :] = k_buf[T:T + WINDOW, :]
    vt_buf[0] = vt_buf[T // WINDOW]

    o_ref[0] = _mix_out(x, ys, yt_buf[...].T, yl, gout, wout_ref[...])

    @pl.when(j == last)
    def _():
        kout_ref[0] = k[T - WINDOW:, :]
        vout_ref[0] = v[T - WINDOW:, :]
        sre_ref[0] = gr
        sim_ref[0] = gi
        convout_ref[0] = tail[SUBLANES - (CONV_WIDTH - 1):, :]
        lruout_ref[0] = gl_state


def _mixer_prompt(x, p):
    bn, L, _ = x.shape
    T = MIX_ROWS
    const2 = lambda b, j: (0, 0)
    vspec = lambda shape: pl.BlockSpec(shape, const2)
    per_b = lambda shape: pl.BlockSpec((1,) + shape, lambda b, j: (b, 0, 0))
    in_specs = [
        pl.BlockSpec((1, T, D_MODEL), lambda b, j: (b, j, 0)),
        vspec((1, D_MODEL)), vspec((D_MODEL, IN_WIDTH)),
        vspec((1, SSM_FLAT)), vspec((1, SSM_FLAT)),
        vspec((SSM_WIDTH, 2 * SSM_FLAT)), vspec((2 * SSM_FLAT, SSM_WIDTH)),
        vspec((1, SSM_WIDTH)), vspec((SSM_WIDTH, SSM_WIDTH)),
        pl.BlockSpec(memory_space=pltpu.SMEM),
        vspec((CONV_WIDTH, LRU_WIDTH)), vspec((1, LRU_WIDTH)),
        vspec((LRU_WIDTH, 2 * LRU_WIDTH)), vspec((1, LRU_WIDTH)), vspec((1, LRU_WIDTH)), vspec((1, LRU_WIDTH)),
        vspec((1, D_MODEL)), vspec((D_MODEL, D_MODEL)), vspec((T, T)),
    ]
    out_specs = [
        pl.BlockSpec((1, T, D_MODEL), lambda b, j: (b, j, 0)),
        per_b((WINDOW, KV_WIDTH)), per_b((WINDOW, KV_WIDTH)),
        per_b((1, SSM_FLAT)), per_b((1, SSM_FLAT)),
        per_b((CONV_WIDTH - 1, LRU_WIDTH)), per_b((1, LRU_WIDTH)),
    ]
    out_shape = [
        jax.ShapeDtypeStruct((bn, L, D_MODEL), F32),
        jax.ShapeDtypeStruct((bn, WINDOW, KV_WIDTH), F32), jax.ShapeDtypeStruct((bn, WINDOW, KV_WIDTH), F32),
        jax.ShapeDtypeStruct((bn, 1, SSM_FLAT), F32), jax.ShapeDtypeStruct((bn, 1, SSM_FLAT), F32),
        jax.ShapeDtypeStruct((bn, CONV_WIDTH - 1, LRU_WIDTH), F32), jax.ShapeDtypeStruct((bn, 1, LRU_WIDTH), F32),
    ]
    planes = lambda width: pltpu.VMEM((width // LANES, T, LANES), F32)
    scratch = [
        pltpu.VMEM((1, SSM_FLAT), F32), pltpu.VMEM((1, SSM_FLAT), F32), pltpu.VMEM((1, LRU_WIDTH), F32),
        planes(SSM_WIDTH), pltpu.VMEM((T, SSM_WIDTH), F32),
        pltpu.VMEM((T, 2 * SSM_FLAT), F32), pltpu.VMEM((T, 2 * SSM_FLAT), BF16),
        pltpu.VMEM((T + SUBLANES, LRU_WIDTH), F32),
        pltpu.VMEM((T, LRU_WIDTH), F32), pltpu.VMEM((T, LRU_WIDTH), F32), pltpu.VMEM((T, LRU_WIDTH), F32),
        pltpu.VMEM((2, SEGS, SSM_FLAT), F32),
        pltpu.VMEM((N_HEADS, T, LANES), BF16),
        pltpu.VMEM((T + WINDOW, KV_WIDTH), BF16),
        pltpu.VMEM((T // WINDOW + 1, KV_WIDTH, WINDOW), BF16),
        pltpu.VMEM((ATT_WIDTH, T), F32),
    ]
    return pl.pallas_call(
        _mixer_prompt_kernel,
        grid=(bn, L // T),
        in_specs=in_specs, out_specs=out_specs, out_shape=out_shape, scratch_shapes=scratch,
        compiler_params=pltpu.CompilerParams(
            dimension_semantics=("arbitrary", "arbitrary"), vmem_limit_bytes=VMEM_LIMIT),
        name="mixer_prompt",
    )(x, p["gmix"], p["win"], p["lb_re"], p["lb_im"], p["bbig"], p["cbig"], p["d"], p["wglu"], p["sink"],
      p["convw"], p["convb"], p["wax"], p["b_a"], p["b_x"], p["lam"], p["gout"], p["wout"], _unpermute_matrix())


def _unpermute_matrix():
    token = jnp.arange(MIX_ROWS)
    source = (token % SEG_LEN) * SEGS + token // SEG_LEN
    return (source[:, None] == jnp.arange(MIX_ROWS)[None, :]).astype(BF16)


def _mixer_sample_kernel(
        x_ref, ck_ref, cv_ref, s0re_ref, s0im_ref, conv0_ref, lru0_ref,
        gmix_ref, win_ref, lbre_ref, lbim_ref, bbig_ref, cbig_ref, d_ref, wglu_ref, sink_ref,
        convw_ref, convb_ref, wax_ref, ba_ref, bx_ref, lam_ref, gout_ref, wout_ref,
        o_ref, kout_ref, vout_ref, sre_ref, sim_ref, convout_ref, lruout_ref,
        xs_buf, xp_buf, a_buf, b_buf, hl_buf, q_buf, ya_buf, *, nb, ns):
    rows = nb * ns
    x = x_ref[...]
    h = _rms(x, gmix_ref[...]).astype(BF16)
    proj = _dot(h, win_ref[...])
    u = proj[:, :OFF_Q]
    k = proj[:, OFF_K:OFF_V]
    v = proj[:, OFF_V:OFF_LX]
    xl = proj[:, OFF_LX:OFF_LG]
    gl = proj[:, OFF_LG:]
    gout = gout_ref[...]
    kout_ref[...] = k
    vout_ref[...] = v

    xs_buf[...] = _dot(u.astype(BF16), bbig_ref[...])
    lr = lbre_ref[...]
    li = lbim_ref[...]
    hr = s0re_ref[...]
    hi = s0im_ref[...]
    for t in range(ns):
        sl = slice(t * nb, (t + 1) * nb)
        nr = lr * hr - li * hi + xs_buf[sl, 0:SSM_FLAT]
        ni = lr * hi + li * hr + xs_buf[sl, SSM_FLAT:2 * SSM_FLAT]
        hr, hi = nr, ni
        xs_buf[sl, 0:SSM_FLAT] = hr
        xs_buf[sl, SSM_FLAT:2 * SSM_FLAT] = hi
    sre_ref[...] = hr
    sim_ref[...] = hi
    y = _dot(xs_buf[...].astype(BF16), cbig_ref[...])
    ys = _s5_post(y, u, d_ref[...], wglu_ref[...], gout[:, :SSM_WIDTH])

    npre = (CONV_WIDTH - 1) * nb
    xp_buf[0:npre, :] = conv0_ref[...]
    xp_buf[npre:npre + rows, :] = xl
    cw = convw_ref[...]
    xc = convb_ref[...] + cw[0:1, :] * xp_buf[0:rows, :]
    for t in range(1, CONV_WIDTH):
        xc = xc + cw[t:t + 1, :] * xp_buf[t * nb:t * nb + rows, :]
    convout_ref[...] = xp_buf[rows:rows + npre, :]
    a, b = _lru_gates(xc, wax_ref[...], ba_ref[...], bx_ref[...], lam_ref[...])
    a_buf[...] = a
    b_buf[...] = b
    hh = lru0_ref[...]
    for t in range(ns):
        sl = slice(t * nb, (t + 1) * nb)
        hh = a_buf[sl, :] * hh + b_buf[sl, :]
        hl_buf[sl, :] = hh
    lruout_ref[...] = hh
    yl = jax.nn.gelu(gl) * hl_buf[...]

    _put(q_buf, proj[:, OFF_Q:OFF_K])

    def attn_stream(bi, _):
        qb = _get_strided(q_buf, bi, ns, nb).astype(BF16)
        kn = kout_ref[pl.ds(bi, ns, stride=nb), :]
        vn = vout_ref[pl.ds(bi, ns, stride=nb), :]
        kk = jnp.concatenate([ck_ref[bi], kn], axis=0).astype(BF16)
        vv = jnp.concatenate([cv_ref[bi], vn], axis=0).astype(BF16)
        _put_strided(ya_buf, bi, ns, nb, _attend(qb, kk, vv, sink_ref, ns, None))
        return 0

    lax.fori_loop(0, nb, attn_stream, 0)
    o_ref[...] = _mix_out(x, ys, _get(ya_buf), yl, gout, wout_ref[...])


def _mixer_sample(x, ck, cv, s0re, s0im, conv0, lru0, p, nb, ns):
    rows = nb * ns
    vm = pl.BlockSpec(memory_space=pltpu.VMEM)
    in_specs = [vm] * 15 + [pl.BlockSpec(memory_space=pltpu.SMEM)] + [vm] * 8
    out_shape = [
        jax.ShapeDtypeStruct((rows, D_MODEL), F32),
        jax.ShapeDtypeStruct((rows, KV_WIDTH), F32), jax.ShapeDtypeStruct((rows, KV_WIDTH), F32),
        jax.ShapeDtypeStruct((nb, SSM_FLAT), F32), jax.ShapeDtypeStruct((nb, SSM_FLAT), F32),
        jax.ShapeDtypeStruct(((CONV_WIDTH - 1) * nb, LRU_WIDTH), F32), jax.ShapeDtypeStruct((nb, LRU_WIDTH), F32),
    ]
    scratch = [
        pltpu.VMEM((rows, 2 * SSM_FLAT), F32),
        pltpu.VMEM((rows + (CONV_WIDTH - 1) * nb, LRU_WIDTH), F32),
        pltpu.VMEM((rows, LRU_WIDTH), F32), pltpu.VMEM((rows, LRU_WIDTH), F32), pltpu.VMEM((rows, LRU_WIDTH), F32),
        pltpu.VMEM((ATT_WIDTH // LANES, rows, LANES), F32),
        pltpu.VMEM((ATT_WIDTH // LANES, rows, LANES), F32),
    ]
    return pl.pallas_call(
        functools.partial(_mixer_sample_kernel, nb=nb, ns=ns),
        in_specs=in_specs, out_specs=[vm] * 7, out_shape=out_shape, scratch_shapes=scratch,
        compiler_params=pltpu.CompilerParams(vmem_limit_bytes=VMEM_LIMIT),
        name="mixer_sample",
    )(x, ck, cv, s0re, s0im, conv0, lru0,
      p["gmix"], p["win"], p["lb_re"], p["lb_im"], p["bbig"], p["cbig"], p["d"], p["wglu"], p["sink"],
      p["convw"], p["convb"], p["wax"], p["b_a"], p["b_x"], p["lam"], p["gout"], p["wout"])


def _block_diag(w):
    n, i, o = w.shape
    eye = jnp.eye(n, dtype=w.dtype)
    return jnp.einsum("nm,nio->nimo", eye, w).reshape(n * i, n * o)


def _layer_params(l, mix_norm, w_in, ssm_a_re, ssm_a_im, ssm_log_dt, ssm_b_re, ssm_b_im, ssm_c_re, ssm_c_im,
                  ssm_d, ssm_w_glu, attn_sink, conv_w, conv_b, lru_w_a, lru_b_a, lru_w_x, lru_b_x, lru_lambda,
                  out_norm, w_out):
    a_re = ssm_a_re[l]
    a_im = ssm_a_im[l]
    dt = jnp.exp(ssm_log_dt[l])[:, None]
    mag = jnp.exp(a_re * dt)
    lb_re = mag * jnp.cos(a_im * dt)
    lb_im = mag * jnp.sin(a_im * dt)
    den = a_re * a_re + a_im * a_im
    nr = lb_re - 1.0
    k_re = (nr * a_re + lb_im * a_im) / den
    k_im = (lb_im * a_re - nr * a_im) / den
    b_re = ssm_b_re[l]
    b_im = ssm_b_im[l]
    bb_re = k_re[..., None] * b_re - k_im[..., None] * b_im
    bb_im = k_re[..., None] * b_im + k_im[..., None] * b_re
    to_gh_p = lambda m: jnp.transpose(m, (0, 2, 1))
    bbig = jnp.concatenate([_block_diag(to_gh_p(bb_re)), _block_diag(to_gh_p(bb_im))], axis=1)
    to_gp_h = lambda m: jnp.transpose(m, (0, 2, 1))
    cbig = jnp.concatenate([_block_diag(to_gp_h(ssm_c_re[l])), -_block_diag(to_gp_h(ssm_c_im[l]))], axis=0)
    row = lambda vec: vec.reshape(1, -1)
    return {
        "gmix": row(mix_norm[l]), "win": w_in[l].astype(BF16),
        "lb_re": row(lb_re), "lb_im": row(lb_im),
        "bbig": bbig.astype(BF16), "cbig": cbig.astype(BF16),
        "d": row(ssm_d[l]), "wglu": ssm_w_glu[l].astype(BF16), "sink": attn_sink[l],
        "convw": conv_w[l], "convb": row(conv_b[l]),
        "wax": jnp.concatenate([_block_diag(lru_w_a[l]), _block_diag(lru_w_x[l])], axis=1).astype(BF16),
        "b_a": row(lru_b_a[l]), "b_x": row(lru_b_x[l]), "lam": row(lru_lambda[l]),
        "gout": row(out_norm[l]), "wout": w_out[l].astype(BF16),
    }


def kernel(x_prompt, x_sample, cache_k, cache_v, state_ssm_re, state_ssm_im, state_conv, state_lru,
           ffn1_norm, ffn1_w_gate, ffn1_w_up, ffn1_w_down, mix_norm, w_in,
           ssm_a_re, ssm_a_im, ssm_log_dt, ssm_b_re, ssm_b_im, ssm_c_re, ssm_c_im, ssm_d, ssm_w_glu,
           attn_sink, conv_w, conv_b, lru_w_a, lru_b_a, lru_w_x, lru_b_x, lru_lambda,
           out_norm, w_out, ffn2_norm, ffn2_w_gate, ffn2_w_up, ffn2_w_down, final_norm):
    bn, L, _ = x_prompt.shape
    nb, ns, _ = x_sample.shape
    row = lambda vec: vec.reshape(1, -1)
    gfin = row(final_norm)

    xp = x_prompt.reshape(bn * L, D_MODEL)
    xs = jnp.transpose(x_sample, (1, 0, 2)).reshape(ns * nb, D_MODEL)
    prompt_states = [[] for _ in range(6)]
    sample_states = [[] for _ in range(6)]
    for l in range(DEPTH):
        p = _layer_params(l, mix_norm, w_in, ssm_a_re, ssm_a_im, ssm_log_dt, ssm_b_re, ssm_b_im, ssm_c_re,
                          ssm_c_im, ssm_d, ssm_w_glu, attn_sink, conv_w, conv_b, lru_w_a, lru_b_a, lru_w_x,
                          lru_b_x, lru_lambda, out_norm, w_out)
        f1 = (l, row(ffn1_norm[l]), ffn1_w_gate, ffn1_w_up, ffn1_w_down)
        f2 = (l, row(ffn2_norm[l]), ffn2_w_gate, ffn2_w_up, ffn2_w_down)
        is_last = l == DEPTH - 1

        xp, xs = _ffn(xp, xs, *f1, gfin, False)
        xp3, kp, vp, srp, sip, cvp, lrp = _mixer_prompt(xp.reshape(bn, L, D_MODEL), p)
        xp = xp3.reshape(bn * L, D_MODEL)
        for lst, s in zip(prompt_states, (
                kp.reshape(bn, WINDOW, N_KV, HEAD_DIM), vp.reshape(bn, WINDOW, N_KV, HEAD_DIM),
                srp.reshape(bn, SSM_GROUPS, SSM_STATE), sip.reshape(bn, SSM_GROUPS, SSM_STATE),
                cvp, lrp.reshape(bn, LRU_WIDTH))):
            lst.append(s)

        ck = cache_k[l].reshape(nb, -1, KV_WIDTH)
        cv = cache_v[l].reshape(nb, -1, KV_WIDTH)
        conv0 = jnp.transpose(state_conv[l], (1, 0, 2)).reshape((CONV_WIDTH - 1) * nb, LRU_WIDTH)
        xs, ks, vs, srs, sis, cvs, lrs = _mixer_sample(
            xs, ck, cv, state_ssm_re[l].reshape(nb, SSM_FLAT), state_ssm_im[l].reshape(nb, SSM_FLAT),
            conv0, state_lru[l], p, nb, ns)
        xp, xs = _ffn(xp, xs, *f2, gfin, is_last)
        unflip = lambda m, w: jnp.transpose(m.reshape(-1, nb, w), (1, 0, 2))
        for lst, s in zip(sample_states, (
                unflip(ks, KV_WIDTH).reshape(nb, ns, N_KV, HEAD_DIM),
                unflip(vs, KV_WIDTH).reshape(nb, ns, N_KV, HEAD_DIM),
                srs.reshape(nb, SSM_GROUPS, SSM_STATE), sis.reshape(nb, SSM_GROUPS, SSM_STATE),
                unflip(cvs, LRU_WIDTH), lrs)):
            lst.append(s)

    y_prompt = xp.reshape(bn, L, D_MODEL)
    y_sample = jnp.transpose(xs.reshape(ns, nb, D_MODEL), (1, 0, 2))
    return (y_prompt, y_sample, *[jnp.stack(c, axis=0) for c in prompt_states],
            *[jnp.stack(c, axis=0) for c in sample_states])
```

```python
import functools
import math

import jax
import jax.numpy as jnp
from jax import lax
from jax.experimental import pallas as pl
from jax.experimental.pallas import tpu as pltpu

F32 = jnp.float32
BF16 = jnp.bfloat16

D_MODEL = 1024
D_FF = 2816
DEPTH = 2
CHUNK = 64
SSM_WIDTH = 256
SSM_GROUP = 16
SSM_GROUPS = 16
SSM_STATE = 64
SSM_FLAT = SSM_GROUPS * SSM_STATE
HEAD_DIM = 64
ATT_WIDTH = 512
N_HEADS = 8
N_KV = 2
Q_PER_KV = 4
KV_WIDTH = 128
WINDOW = 128
LRU_WIDTH = 256
LRU_BLOCKS = 4
LRU_BLOCK = 64
CONV_WIDTH = 4
LRU_C = 8.0
OFF_Q = SSM_WIDTH
OFF_K = OFF_Q + ATT_WIDTH
OFF_V = OFF_K + KV_WIDTH
OFF_LX = OFF_V + KV_WIDTH
OFF_LG = OFF_LX + LRU_WIDTH
IN_WIDTH = OFF_LG + LRU_WIDTH
ATT_SCALE = HEAD_DIM ** -0.5
assert math.frexp(ATT_SCALE)[0] == 0.5, "the prompt mixer pre-scales q, exact only for a power-of-two scale"
EPS = 1e-6
NEG_INF = -1e30

SUBLANES = 8
LANES = 128
FFN_ROWS = 512
FFN_COLS = 256
STAGE_SLOTS = 4
STAGE_IN_ROWS = 128
STAGE_OUT_ROWS = 256
MIX_ROWS = 512
PROJ_COLS = 256
SEGS = SUBLANES
SEG_LEN = MIX_ROWS // SEGS
VMEM_LIMIT = 56 * 1024 * 1024


def _rms(x, g):
    return x * lax.rsqrt(jnp.mean(x * x, axis=-1, keepdims=True) + EPS) * g


def _dot(a, b):
    return jnp.dot(a, b, preferred_element_type=F32)


def _dot_row_halves(a, b):
    half = a.shape[0] // 2
    return jnp.concatenate([_dot(a[:half], b), _dot(a[half:], b)], axis=0)


def _cmul(ar, ai, br, bi):
    return ar * br - ai * bi, ar * bi + ai * br


def _put(ref, val):
    for c in range(ref.shape[0]):
        ref[c] = val[:, c * LANES:(c + 1) * LANES]


def _get(ref):
    return jnp.concatenate([ref[c] for c in range(ref.shape[0])], axis=1)


def _get_strided(ref, start, n, stride):
    return jnp.concatenate(
        [ref[c, pl.ds(start, n, stride=stride), :] for c in range(ref.shape[0])], axis=1)


def _put_strided(ref, start, n, stride, val):
    for c in range(ref.shape[0]):
        ref[c, pl.ds(start, n, stride=stride), :] = val[:, c * LANES:(c + 1) * LANES]


def _weight_jobs(layer, wg_hbm, wu_hbm, wd_hbm, wg_v, wu_v, wd_v, stage_in, stage_out, sem):
    jobs = []
    used = [0, 0]

    def add(src, dst, stage, kind, r0, n):
        slot = used[kind] % STAGE_SLOTS
        used[kind] += 1
        copy = pltpu.make_async_copy(src.at[layer, pl.ds(r0, n), :], stage.at[slot], sem.at[kind, slot])
        jobs.append((copy, stage.at[slot], dst, r0, n))

    for r0 in range(0, D_MODEL, STAGE_IN_ROWS):
        add(wg_hbm, wg_v, stage_in, 0, r0, STAGE_IN_ROWS)
        add(wu_hbm, wu_v, stage_in, 0, r0, STAGE_IN_ROWS)
    for r0 in range(0, D_FF, STAGE_OUT_ROWS):
        add(wd_hbm, wd_v, stage_out, 1, r0, STAGE_OUT_ROWS)
    return jobs


def _ffn_kernel(xp_ref, xs_ref, g_ref, wg_hbm, wu_hbm, wd_hbm, gf_ref, op_ref, os_ref,
                wg_ref, wu_ref, wd_ref, stage_in, stage_out, sem, *, layer, final_norm, prompt_tiles):
    i = pl.program_id(0)

    @pl.when(i == 0)
    def _():
        jobs = _weight_jobs(layer, wg_hbm, wu_hbm, wd_hbm, wg_ref, wu_ref, wd_ref, stage_in, stage_out, sem)
        ahead = STAGE_SLOTS - 1
        for k in range(min(ahead, len(jobs))):
            jobs[k][0].start()
        for k, (copy, staged, dst, r0, n) in enumerate(jobs):
            if k + ahead < len(jobs):
                jobs[k + ahead][0].start()
            copy.wait()
            dst[r0:r0 + n, :] = staged[...].astype(BF16)

    x = jnp.where(i < prompt_tiles, xp_ref[...], xs_ref[...])
    h = _rms(x, g_ref[...]).astype(BF16)
    acc = jnp.zeros(x.shape, F32)
    for c in range(D_FF // FFN_COLS):
        sl = slice(c * FFN_COLS, (c + 1) * FFN_COLS)
        g = _dot(h, wg_ref[:, sl])
        u = _dot(h, wu_ref[:, sl])
        a = (g * jax.nn.sigmoid(g) * u).astype(BF16)
        acc = acc + _dot(a, wd_ref[sl, :])
    y = x + 0.5 * acc
    if final_norm:
        y = _rms(y, gf_ref[...])

    @pl.when(i < prompt_tiles)
    def _():
        op_ref[...] = y

    @pl.when(i >= prompt_tiles)
    def _():
        os_ref[...] = y


def _ffn(xp, xs, layer, g, wg, wu, wd, gf, final_norm):
    prompt_tiles = xp.shape[0] // FFN_ROWS
    assert xs.shape[0] == FFN_ROWS
    const = lambda i: (0, 0)
    prompt_map = lambda i: (jnp.minimum(i, prompt_tiles - 1), 0)
    hbm = pl.BlockSpec(memory_space=pl.ANY)
    return pl.pallas_call(
        functools.partial(_ffn_kernel, layer=layer, final_norm=final_norm, prompt_tiles=prompt_tiles),
        grid=(prompt_tiles + 1,),
        in_specs=[
            pl.BlockSpec((FFN_ROWS, D_MODEL), prompt_map),
            pl.BlockSpec((FFN_ROWS, D_MODEL), const),
            pl.BlockSpec((1, D_MODEL), const),
            hbm, hbm, hbm,
            pl.BlockSpec((1, D_MODEL), const),
        ],
        out_specs=[pl.BlockSpec((FFN_ROWS, D_MODEL), prompt_map), pl.BlockSpec((FFN_ROWS, D_MODEL), const)],
        out_shape=[jax.ShapeDtypeStruct(xp.shape, F32), jax.ShapeDtypeStruct(xs.shape, F32)],
        scratch_shapes=[
            pltpu.VMEM((D_MODEL, D_FF), BF16), pltpu.VMEM((D_MODEL, D_FF), BF16), pltpu.VMEM((D_FF, D_MODEL), BF16),
            pltpu.VMEM((STAGE_SLOTS, STAGE_IN_ROWS, D_FF), F32),
            pltpu.VMEM((STAGE_SLOTS, STAGE_OUT_ROWS, D_MODEL), F32),
            pltpu.SemaphoreType.DMA((2, STAGE_SLOTS)),
        ],
        compiler_params=pltpu.CompilerParams(
            dimension_semantics=("arbitrary",), vmem_limit_bytes=VMEM_LIMIT),
        name="ffn",
    )(xp, xs, g, wg, wu, wd, gf)


def _gelu(x):
    z = math.sqrt(2.0 / math.pi) * (x + 0.044715 * (x * x * x))
    return x * jax.nn.sigmoid(2.0 * z)


def _log_sigmoid(x):
    return jnp.minimum(x, 0.0) - jnp.log1p(jnp.exp(-jnp.abs(x)))


def _s5_post(y, u, d, wglu, g):
    z = _gelu(y + d * u)
    out = z * jax.nn.sigmoid(_dot_row_halves(z.astype(BF16), wglu))
    return _rms(out, g)


def _lru_gates(xc, wax, b_a, b_x, lam):
    ga = _dot(xc.astype(BF16), wax)
    r = jax.nn.sigmoid(ga[:, :LRU_WIDTH] + b_a)
    i = jax.nn.sigmoid(ga[:, LRU_WIDTH:] + b_x)
    log_a = LRU_C * r * _log_sigmoid(lam)
    a = jnp.exp(log_a)
    t = jnp.tanh(log_a)
    mult = jnp.sqrt(-2.0 * t / (1.0 - t))
    return a, mult * i * xc


def _sink_softmax(s, sink_col):
    m = jnp.maximum(jnp.max(s, axis=-1, keepdims=True), sink_col)
    e = jnp.exp(s - m)
    den = jnp.sum(e, axis=-1, keepdims=True) + jnp.exp(sink_col - m)
    return e / den


def _attend(q_rows, kk, vv, sink_ref, nq, valid):
    outs = []
    for kvh in range(N_KV):
        qs = jnp.concatenate(
            [q_rows[:, (kvh * Q_PER_KV + g) * HEAD_DIM:(kvh * Q_PER_KV + g + 1) * HEAD_DIM]
             for g in range(Q_PER_KV)], axis=0)
        kh = kk[:, kvh * HEAD_DIM:(kvh + 1) * HEAD_DIM]
        vh = vv[:, kvh * HEAD_DIM:(kvh + 1) * HEAD_DIM]
        s = lax.dot_general(qs, kh, (((1,), (1,)), ((), ())), preferred_element_type=F32) * ATT_SCALE
        if valid is not None:
            s = jnp.where(valid, s, NEG_INF)
        sink_col = jnp.concatenate(
            [jnp.full((nq, 1), sink_ref[kvh * Q_PER_KV + g], F32) for g in range(Q_PER_KV)], axis=0)
        p = _sink_softmax(s, sink_col).astype(BF16)
        o = _dot(p, vh)
        outs.extend(o[g * nq:(g + 1) * nq, :] for g in range(Q_PER_KV))
    return jnp.concatenate(outs, axis=1)


def _mix_out(x, ys, ya, yl, gout, wout):
    ycat = jnp.concatenate([
        ys,
        _rms(ya, gout[:, SSM_WIDTH:SSM_WIDTH + ATT_WIDTH]),
        _rms(yl, gout[:, SSM_WIDTH + ATT_WIDTH:]),
    ], axis=1).astype(BF16)
    return x + _dot(ycat, wout)


def _mixer_prompt_kernel(
        x_ref, xnext_ref, gmix_ref, win_ref, lbre_ref, lbim_ref, bbig_ref, cbig_ref, d_ref, wglu_ref, sink_ref,
        convw_ref, convb_ref, wax_ref, ba_ref, bx_ref, lam_ref, gout_ref, wout_ref, unperm_ref,
        o_ref, kout_ref, vout_ref, sre_ref, sim_ref, convout_ref, lruout_ref,
        st_re, st_im, st_lru, u_buf, up_buf, xs_buf, hs_buf, xl_buf, a_buf, b_buf, hl_buf, lb_buf,
        qz_buf, k_buf, vt_buf, yt_buf, proj_buf, hnext_buf):
    j = pl.program_id(1)
    last = pl.num_programs(1) - 1
    T = MIX_ROWS

    @pl.when(j == 0)
    def _():
        st_re[...] = jnp.zeros_like(st_re)
        st_im[...] = jnp.zeros_like(st_im)
        st_lru[...] = jnp.zeros_like(st_lru)
        xl_buf[0:SUBLANES, :] = jnp.zeros((SUBLANES, LRU_WIDTH), F32)
        k_buf[0:WINDOW, :] = jnp.zeros((WINDOW, KV_WIDTH), BF16)
        vt_buf[0] = jnp.zeros((KV_WIDTH, WINDOW), BF16)

    @pl.when(jnp.logical_and(pl.program_id(0) == 0, j == 0))
    def _():
        proj_buf[...] = _dot(_rms(x_ref[0], gmix_ref[...]).astype(BF16), win_ref[...])

    x = x_ref[0]
    proj = proj_buf[...]
    hnext_buf[...] = _rms(xnext_ref[0], gmix_ref[...]).astype(BF16)
    pending = list(range(0, IN_WIDTH, PROJ_COLS))

    def project_next_block():
        c0 = pending.pop(0)
        proj_buf[:, c0:c0 + PROJ_COLS] = _dot(hnext_buf[...], win_ref[:, c0:c0 + PROJ_COLS])

    u = proj[:, :OFF_Q]
    k = proj[:, OFF_K:OFF_V]
    v = proj[:, OFF_V:OFF_LX]
    xl = proj[:, OFF_LX:OFF_LG]
    gl = proj[:, OFF_LG:]
    gout = gout_ref[...]

    _put(u_buf, u)
    for i in range(SEG_LEN):
        up_buf[i * SEGS:(i + 1) * SEGS, :] = _get_strided(u_buf, i, SEGS, SEG_LEN)
    up = up_buf[...]
    xs_buf[...] = _dot(up.astype(BF16), bbig_ref[...])
    lr = lbre_ref[...]
    li = lbim_ref[...]
    lb_buf[0] = jnp.broadcast_to(lr, (SEGS, SSM_FLAT))
    lb_buf[1] = jnp.broadcast_to(li, (SEGS, SSM_FLAT))

    def s5_two_steps(ii, carry, store):
        hr, hi = carry
        r0 = ii * 2 * SEGS
        lrb = lb_buf[0]
        lib = lb_buf[1]
        new_r, new_i = [], []
        for half in range(2):
            xr = xs_buf[pl.ds(r0 + half * SEGS, SEGS), 0:SSM_FLAT]
            xi = xs_buf[pl.ds(r0 + half * SEGS, SEGS), SSM_FLAT:2 * SSM_FLAT]
            hr, hi = lrb * hr - lib * hi + xr, lrb * hi + lib * hr + xi
            new_r.append(hr)
            new_i.append(hi)
        if store:
            hs_buf[pl.ds(r0, 2 * SEGS), 0:SSM_FLAT] = jnp.concatenate(new_r, axis=0).astype(BF16)
            hs_buf[pl.ds(r0, 2 * SEGS), SSM_FLAT:2 * SSM_FLAT] = jnp.concatenate(new_i, axis=0).astype(BF16)
        return hr, hi

    zero_seg = jnp.zeros((SEGS, SSM_FLAT), F32)
    pairs = SEG_LEN // 2
    state = (zero_seg, zero_seg)
    for ii in range(pairs):
        if ii in (0, pairs // 2):
            project_next_block()
        state = s5_two_steps(ii, state, store=False)
    loc_r, loc_i = state
    pr, pi = lr, li
    for _ in range(int(math.log2(SEG_LEN))):
        pr, pi = _cmul(pr, pi, pr, pi)
    gr, gi = st_re[...], st_im[...]
    rows_r, rows_i = [], []
    for s in range(SEGS):
        rows_r.append(gr)
        rows_i.append(gi)
        ar, ai = _cmul(pr, pi, gr, gi)
        gr = ar + loc_r[s:s + 1, :]
        gi = ai + loc_i[s:s + 1, :]
    st_re[...] = gr
    st_im[...] = gi
    state = (jnp.concatenate(rows_r, axis=0), jnp.concatenate(rows_i, axis=0))
    for ii in range(pairs):
        if ii in (0, pairs // 2):
            project_next_block()
        state = s5_two_steps(ii, state, store=True)
    y = _dot_row_halves(hs_buf[...], cbig_ref[...])
    ys_regrouped = _s5_post(y, up, d_ref[...], wglu_ref[...], gout[:, :SSM_WIDTH]).astype(BF16)
    ys = _dot_row_halves(unperm_ref[...], ys_regrouped)

    xl_buf[SUBLANES:SUBLANES + T, :] = xl
    cw = convw_ref[...]
    xc = convb_ref[...] + cw[0:1, :] * xl_buf[SUBLANES - 3:SUBLANES - 3 + T, :]
    for t in range(1, CONV_WIDTH):
        xc = xc + cw[t:t + 1, :] * xl_buf[SUBLANES - 3 + t:SUBLANES - 3 + t + T, :]
    tail = xl_buf[T:T + SUBLANES, :]
    xl_buf[0:SUBLANES, :] = tail
    project_next_block()
    a, b = _lru_gates(xc, wax_ref[...], ba_ref[...], bx_ref[...], lam_ref[...])
    a3 = a.reshape(T // SUBLANES, SUBLANES, LRU_WIDTH)
    b3 = b.reshape(T // SUBLANES, SUBLANES, LRU_WIDTH)
    slab_row = lax.broadcasted_iota(jnp.int32, a3.shape, 1)
    shift = 1
    while shift < SUBLANES:
        keep = slab_row >= shift
        b3 = jnp.where(keep, a3 * pltpu.roll(b3, shift, axis=1) + b3, b3)
        a3 = jnp.where(keep, a3 * pltpu.roll(a3, shift, axis=1), a3)
        shift *= 2
    a_buf[...] = a3.reshape(T, LRU_WIDTH)
    b_buf[...] = b3.reshape(T, LRU_WIDTH)
    project_next_block()

    def lru_slab(i, carry):
        r0 = pl.multiple_of(i * SUBLANES, SUBLANES)
        hh = a_buf[pl.ds(r0, SUBLANES), :] * carry + b_buf[pl.ds(r0, SUBLANES), :]
        hl_buf[pl.ds(r0, SUBLANES), :] = hh
        return jnp.broadcast_to(hh[SUBLANES - 1:SUBLANES, :], (SUBLANES, LRU_WIDTH))

    lru_carry = lax.fori_loop(0, T // SUBLANES, lru_slab,
                              jnp.broadcast_to(st_lru[...], (SUBLANES, LRU_WIDTH)), unroll=True)
    gl_state = lru_carry[0:1, :]
    st_lru[...] = gl_state
    yl = _gelu(gl) * hl_buf[...]

    lo = lax.broadcasted_iota(jnp.int32, (T, LANES), 1) < HEAD_DIM
    for jp in range(N_HEADS // 2):
        pair = proj[:, OFF_Q + jp * LANES:OFF_Q + (jp + 1) * LANES] * ATT_SCALE
        swapped = pltpu.roll(pair, HEAD_DIM, axis=1)
        if jp < N_HEADS // 4:
            even, odd = jnp.where(lo, pair, 0.0), jnp.where(lo, swapped, 0.0)
        else:
            even, odd = jnp.where(lo, 0.0, swapped), jnp.where(lo, 0.0, pair)
        qz_buf[2 * jp] = even.astype(BF16)
        qz_buf[2 * jp + 1] = odd.astype(BF16)
    k_buf[WINDOW:WINDOW + T, :] = k.astype(BF16)
    vt = v.T.astype(BF16)
    for i in range(T // WINDOW):
        vt_buf[i + 1] = vt[:, i * WINDOW:(i + 1) * WINDOW]

    span = 2 * CHUNK
    group = Q_PER_KV * span
    krow = lax.broadcasted_iota(jnp.int32, (2 * span, group), 0)
    qcol = lax.broadcasted_iota(jnp.int32, (2 * span, group), 1) % span
    band = jnp.logical_or(jnp.logical_and(qcol < CHUNK, krow < WINDOW + CHUNK),
                          jnp.logical_and(qcol >= CHUNK, krow >= CHUNK))
    for m in range(T // span):
        valid = band if m > 0 else jnp.logical_and(band, jnp.logical_or(krow >= WINDOW, j > 0))
        top, bot = (CHUNK if m > 0 else WINDOW), WINDOW + CHUNK
        kwin = k_buf[m * span:(m + 2) * span, :]
        vtwin = jnp.concatenate([vt_buf[m], vt_buf[m + 1]], axis=1)
        for kvh in range(N_KV):
            heads = range(kvh * Q_PER_KV, (kvh + 1) * Q_PER_KV)
            qz = jnp.concatenate([qz_buf[h, m * span:(m + 1) * span, :] for h in heads], axis=0)
            s = lax.dot_general(kwin, qz, (((1,), (1,)), ((), ())), preferred_element_type=F32)
            s = jnp.concatenate([jnp.where(valid[:top], s[:top], NEG_INF), s[top:bot],
                                 jnp.where(valid[bot:], s[bot:], NEG_INF)], axis=0)
            sink_row = jnp.concatenate([jnp.full((1, span), sink_ref[h], F32) for h in heads], axis=1)
            mx = jnp.maximum(jnp.max(s, axis=0, keepdims=True), sink_row)
            e = jnp.exp(s - mx)
            den = jnp.sum(e, axis=0, keepdims=True) + jnp.exp(sink_row - mx)
            o = _dot(vtwin[kvh * HEAD_DIM:(kvh + 1) * HEAD_DIM, :], e.astype(BF16)) * (1.0 / den)
            for g, h in enumerate(heads):
                yt_buf[h * HEAD_DIM:(h + 1) * HEAD_DIM, m * span:(m + 1) * span] = o[:, g * span:(g + 1) * span]
    k_buf[0:WINDOW, :] = k_buf[T:T + WINDOW, :]
    vt_buf[0] = vt_buf[T // WINDOW]

    o_ref[0] = _mix_out(x, ys, yt_buf[...].T, yl, gout, wout_ref[...])
    while pending:
        project_next_block()

    @pl.when(j == last)
    def _():
        kout_ref[0] = k[T - WINDOW:, :]
        vout_ref[0] = v[T - WINDOW:, :]
        sre_ref[0] = gr
        sim_ref[0] = gi
        convout_ref[0] = tail[SUBLANES - (CONV_WIDTH - 1):, :]
        lruout_ref[0] = gl_state


def _mixer_prompt(x, p):
    bn, L, _ = x.shape
    T = MIX_ROWS
    const2 = lambda b, j: (0, 0)
    vspec = lambda shape: pl.BlockSpec(shape, const2)
    per_b = lambda shape: pl.BlockSpec((1,) + shape, lambda b, j: (b, 0, 0))
    tiles = L // T

    def next_tile(b, j):
        n = jnp.minimum(b * tiles + j + 1, bn * tiles - 1)
        return (n // tiles, n % tiles, 0)

    in_specs = [
        pl.BlockSpec((1, T, D_MODEL), lambda b, j: (b, j, 0)),
        pl.BlockSpec((1, T, D_MODEL), next_tile),
        vspec((1, D_MODEL)), vspec((D_MODEL, IN_WIDTH)),
        vspec((1, SSM_FLAT)), vspec((1, SSM_FLAT)),
        vspec((SSM_WIDTH, 2 * SSM_FLAT)), vspec((2 * SSM_FLAT, SSM_WIDTH)),
        vspec((1, SSM_WIDTH)), vspec((SSM_WIDTH, SSM_WIDTH)),
        pl.BlockSpec(memory_space=pltpu.SMEM),
        vspec((CONV_WIDTH, LRU_WIDTH)), vspec((1, LRU_WIDTH)),
        vspec((LRU_WIDTH, 2 * LRU_WIDTH)), vspec((1, LRU_WIDTH)), vspec((1, LRU_WIDTH)), vspec((1, LRU_WIDTH)),
        vspec((1, D_MODEL)), vspec((D_MODEL, D_MODEL)), vspec((T, T)),
    ]
    out_specs = [
        pl.BlockSpec((1, T, D_MODEL), lambda b, j: (b, j, 0)),
        per_b((WINDOW, KV_WIDTH)), per_b((WINDOW, KV_WIDTH)),
        per_b((1, SSM_FLAT)), per_b((1, SSM_FLAT)),
        per_b((CONV_WIDTH - 1, LRU_WIDTH)), per_b((1, LRU_WIDTH)),
    ]
    out_shape = [
        jax.ShapeDtypeStruct((bn, L, D_MODEL), F32),
        jax.ShapeDtypeStruct((bn, WINDOW, KV_WIDTH), F32), jax.ShapeDtypeStruct((bn, WINDOW, KV_WIDTH), F32),
        jax.ShapeDtypeStruct((bn, 1, SSM_FLAT), F32), jax.ShapeDtypeStruct((bn, 1, SSM_FLAT), F32),
        jax.ShapeDtypeStruct((bn, CONV_WIDTH - 1, LRU_WIDTH), F32), jax.ShapeDtypeStruct((bn, 1, LRU_WIDTH), F32),
    ]
    planes = lambda width: pltpu.VMEM((width // LANES, T, LANES), F32)
    scratch = [
        pltpu.VMEM((1, SSM_FLAT), F32), pltpu.VMEM((1, SSM_FLAT), F32), pltpu.VMEM((1, LRU_WIDTH), F32),
        planes(SSM_WIDTH), pltpu.VMEM((T, SSM_WIDTH), F32),
        pltpu.VMEM((T, 2 * SSM_FLAT), F32), pltpu.VMEM((T, 2 * SSM_FLAT), BF16),
        pltpu.VMEM((T + SUBLANES, LRU_WIDTH), F32),
        pltpu.VMEM((T, LRU_WIDTH), F32), pltpu.VMEM((T, LRU_WIDTH), F32), pltpu.VMEM((T, LRU_WIDTH), F32),
        pltpu.VMEM((2, SEGS, SSM_FLAT), F32),
        pltpu.VMEM((N_HEADS, T, LANES), BF16),
        pltpu.VMEM((T + WINDOW, KV_WIDTH), BF16),
        pltpu.VMEM((T // WINDOW + 1, KV_WIDTH, WINDOW), BF16),
        pltpu.VMEM((ATT_WIDTH, T), F32),
        pltpu.VMEM((T, IN_WIDTH), F32), pltpu.VMEM((T, D_MODEL), BF16),
    ]
    return pl.pallas_call(
        _mixer_prompt_kernel,
        grid=(bn, L // T),
        in_specs=in_specs, out_specs=out_specs, out_shape=out_shape, scratch_shapes=scratch,
        compiler_params=pltpu.CompilerParams(
            dimension_semantics=("arbitrary", "arbitrary"), vmem_limit_bytes=VMEM_LIMIT),
        name="mixer_prompt",
    )(x, x, p["gmix"], p["win"], p["lb_re"], p["lb_im"], p["bbig"], p["cbig"], p["d"], p["wglu"], p["sink"],
      p["convw"], p["convb"], p["wax"], p["b_a"], p["b_x"], p["lam"], p["gout"], p["wout"], _unpermute_matrix())


def _unpermute_matrix():
    token = jnp.arange(MIX_ROWS)
    source = (token % SEG_LEN) * SEGS + token // SEG_LEN
    return (source[:, None] == jnp.arange(MIX_ROWS)[None, :]).astype(BF16)


def _mixer_sample_kernel(
        x_ref, ck_ref, cv_ref, s0re_ref, s0im_ref, conv0_ref, lru0_ref,
        gmix_ref, win_ref, lbre_ref, lbim_ref, bbig_ref, cbig_ref, d_ref, wglu_ref, sink_ref,
        convw_ref, convb_ref, wax_ref, ba_ref, bx_ref, lam_ref, gout_ref, wout_ref,
        o_ref, kout_ref, vout_ref, sre_ref, sim_ref, convout_ref, lruout_ref,
        xs_buf, xp_buf, a_buf, b_buf, hl_buf, q_buf, ya_buf, *, nb, ns):
    rows = nb * ns
    x = x_ref[...]
    h = _rms(x, gmix_ref[...]).astype(BF16)
    proj = _dot(h, win_ref[...])
    u = proj[:, :OFF_Q]
    k = proj[:, OFF_K:OFF_V]
    v = proj[:, OFF_V:OFF_LX]
    xl = proj[:, OFF_LX:OFF_LG]
    gl = proj[:, OFF_LG:]
    gout = gout_ref[...]
    kout_ref[...] = k
    vout_ref[...] = v

    xs_buf[...] = _dot(u.astype(BF16), bbig_ref[...])
    lr = lbre_ref[...]
    li = lbim_ref[...]
    hr = s0re_ref[...]
    hi = s0im_ref[...]
    for t in range(ns):
        sl = slice(t * nb, (t + 1) * nb)
        nr = lr * hr - li * hi + xs_buf[sl, 0:SSM_FLAT]
        ni = lr * hi + li * hr + xs_buf[sl, SSM_FLAT:2 * SSM_FLAT]
        hr, hi = nr, ni
        xs_buf[sl, 0:SSM_FLAT] = hr
        xs_buf[sl, SSM_FLAT:2 * SSM_FLAT] = hi
    sre_ref[...] = hr
    sim_ref[...] = hi
    y = _dot_row_halves(xs_buf[...].astype(BF16), cbig_ref[...])
    ys = _s5_post(y, u, d_ref[...], wglu_ref[...], gout[:, :SSM_WIDTH])

    npre = (CONV_WIDTH - 1) * nb
    xp_buf[0:npre, :] = conv0_ref[...]
    xp_buf[npre:npre + rows, :] = xl
    cw = convw_ref[...]
    xc = convb_ref[...] + cw[0:1, :] * xp_buf[0:rows, :]
    for t in range(1, CONV_WIDTH):
        xc = xc + cw[t:t + 1, :] * xp_buf[t * nb:t * nb + rows, :]
    convout_ref[...] = xp_buf[rows:rows + npre, :]
    a, b = _lru_gates(xc, wax_ref[...], ba_ref[...], bx_ref[...], lam_ref[...])
    a_buf[...] = a
    b_buf[...] = b
    hh = lru0_ref[...]
    for t in range(ns):
        sl = slice(t * nb, (t + 1) * nb)
        hh = a_buf[sl, :] * hh + b_buf[sl, :]
        hl_buf[sl, :] = hh
    lruout_ref[...] = hh
    yl = _gelu(gl) * hl_buf[...]

    _put(q_buf, proj[:, OFF_Q:OFF_K])

    def attn_stream(bi, _):
        qb = _get_strided(q_buf, bi, ns, nb).astype(BF16)
        kn = kout_ref[pl.ds(bi, ns, stride=nb), :]
        vn = vout_ref[pl.ds(bi, ns, stride=nb), :]
        kk = jnp.concatenate([ck_ref[bi], kn], axis=0).astype(BF16)
        vv = jnp.concatenate([cv_ref[bi], vn], axis=0).astype(BF16)
        _put_strided(ya_buf, bi, ns, nb, _attend(qb, kk, vv, sink_ref, ns, None))
        return 0

    lax.fori_loop(0, nb, attn_stream, 0)
    o_ref[...] = _mix_out(x, ys, _get(ya_buf), yl, gout, wout_ref[...])


def _mixer_sample(x, ck, cv, s0re, s0im, conv0, lru0, p, nb, ns):
    rows = nb * ns
    vm = pl.BlockSpec(memory_space=pltpu.VMEM)
    in_specs = [vm] * 15 + [pl.BlockSpec(memory_space=pltpu.SMEM)] + [vm] * 8
    out_shape = [
        jax.ShapeDtypeStruct((rows, D_MODEL), F32),
        jax.ShapeDtypeStruct((rows, KV_WIDTH), F32), jax.ShapeDtypeStruct((rows, KV_WIDTH), F32),
        jax.ShapeDtypeStruct((nb, SSM_FLAT), F32), jax.ShapeDtypeStruct((nb, SSM_FLAT), F32),
        jax.ShapeDtypeStruct(((CONV_WIDTH - 1) * nb, LRU_WIDTH), F32), jax.ShapeDtypeStruct((nb, LRU_WIDTH), F32),
    ]
    scratch = [
        pltpu.VMEM((rows, 2 * SSM_FLAT), F32),
        pltpu.VMEM((rows + (CONV_WIDTH - 1) * nb, LRU_WIDTH), F32),
        pltpu.VMEM((rows, LRU_WIDTH), F32), pltpu.VMEM((rows, LRU_WIDTH), F32), pltpu.VMEM((rows, LRU_WIDTH), F32),
        pltpu.VMEM((ATT_WIDTH // LANES, rows, LANES), F32),
        pltpu.VMEM((ATT_WIDTH // LANES, rows, LANES), F32),
    ]
    return pl.pallas_call(
        functools.partial(_mixer_sample_kernel, nb=nb, ns=ns),
        in_specs=in_specs, out_specs=[vm] * 7, out_shape=out_shape, scratch_shapes=scratch,
        compiler_params=pltpu.CompilerParams(vmem_limit_bytes=VMEM_LIMIT),
        name="mixer_sample",
    )(x, ck, cv, s0re, s0im, conv0, lru0,
      p["gmix"], p["win"], p["lb_re"], p["lb_im"], p["bbig"], p["cbig"], p["d"], p["wglu"], p["sink"],
      p["convw"], p["convb"], p["wax"], p["b_a"], p["b_x"], p["lam"], p["gout"], p["wout"])


def _block_diag(w):
    n, i, o = w.shape
    eye = jnp.eye(n, dtype=w.dtype)
    return jnp.einsum("nm,nio->nimo", eye, w).reshape(n * i, n * o)


def _layer_params(l, mix_norm, w_in, ssm_a_re, ssm_a_im, ssm_log_dt, ssm_b_re, ssm_b_im, ssm_c_re, ssm_c_im,
                  ssm_d, ssm_w_glu, attn_sink, conv_w, conv_b, lru_w_a, lru_b_a, lru_w_x, lru_b_x, lru_lambda,
                  out_norm, w_out):
    a_re = ssm_a_re[l]
    a_im = ssm_a_im[l]
    dt = jnp.exp(ssm_log_dt[l])[:, None]
    mag = jnp.exp(a_re * dt)
    lb_re = mag * jnp.cos(a_im * dt)
    lb_im = mag * jnp.sin(a_im * dt)
    den = a_re * a_re + a_im * a_im
    nr = lb_re - 1.0
    k_re = (nr * a_re + lb_im * a_im) / den
    k_im = (lb_im * a_re - nr * a_im) / den
    b_re = ssm_b_re[l]
    b_im = ssm_b_im[l]
    bb_re = k_re[..., None] * b_re - k_im[..., None] * b_im
    bb_im = k_re[..., None] * b_im + k_im[..., None] * b_re
    to_gh_p = lambda m: jnp.transpose(m, (0, 2, 1))
    bbig = jnp.concatenate([_block_diag(to_gh_p(bb_re)), _block_diag(to_gh_p(bb_im))], axis=1)
    to_gp_h = lambda m: jnp.transpose(m, (0, 2, 1))
    cbig = jnp.concatenate([_block_diag(to_gp_h(ssm_c_re[l])), -_block_diag(to_gp_h(ssm_c_im[l]))], axis=0)
    row = lambda vec: vec.reshape(1, -1)
    return {
        "gmix": row(mix_norm[l]), "win": w_in[l].astype(BF16),
        "lb_re": row(lb_re), "lb_im": row(lb_im),
        "bbig": bbig.astype(BF16), "cbig": cbig.astype(BF16),
        "d": row(ssm_d[l]), "wglu": ssm_w_glu[l].astype(BF16), "sink": attn_sink[l],
        "convw": conv_w[l], "convb": row(conv_b[l]),
        "wax": jnp.concatenate([_block_diag(lru_w_a[l]), _block_diag(lru_w_x[l])], axis=1).astype(BF16),
        "b_a": row(lru_b_a[l]), "b_x": row(lru_b_x[l]), "lam": row(lru_lambda[l]),
        "gout": row(out_norm[l]), "wout": w_out[l].astype(BF16),
    }


def kernel(x_prompt, x_sample, cache_k, cache_v, state_ssm_re, state_ssm_im, state_conv, state_lru,
           ffn1_norm, ffn1_w_gate, ffn1_w_up, ffn1_w_down, mix_norm, w_in,
           ssm_a_re, ssm_a_im, ssm_log_dt, ssm_b_re, ssm_b_im, ssm_c_re, ssm_c_im, ssm_d, ssm_w_glu,
           attn_sink, conv_w, conv_b, lru_w_a, lru_b_a, lru_w_x, lru_b_x, lru_lambda,
           out_norm, w_out, ffn2_norm, ffn2_w_gate, ffn2_w_up, ffn2_w_down, final_norm):
    bn, L, _ = x_prompt.shape
    nb, ns, _ = x_sample.shape
    row = lambda vec: vec.reshape(1, -1)
    gfin = row(final_norm)

    xp = x_prompt.reshape(bn * L, D_MODEL)
    xs = jnp.transpose(x_sample, (1, 0, 2)).reshape(ns * nb, D_MODEL)
    prompt_states = [[] for _ in range(6)]
    sample_states = [[] for _ in range(6)]
    for l in range(DEPTH):
        p = _layer_params(l, mix_norm, w_in, ssm_a_re, ssm_a_im, ssm_log_dt, ssm_b_re, ssm_b_im, ssm_c_re,
                          ssm_c_im, ssm_d, ssm_w_glu, attn_sink, conv_w, conv_b, lru_w_a, lru_b_a, lru_w_x,
                          lru_b_x, lru_lambda, out_norm, w_out)
        f1 = (l, row(ffn1_norm[l]), ffn1_w_gate, ffn1_w_up, ffn1_w_down)
        f2 = (l, row(ffn2_norm[l]), ffn2_w_gate, ffn2_w_up, ffn2_w_down)
        is_last = l == DEPTH - 1

        xp, xs = _ffn(xp, xs, *f1, gfin, False)
        xp3, kp, vp, srp, sip, cvp, lrp = _mixer_prompt(xp.reshape(bn, L, D_MODEL), p)
        xp = xp3.reshape(bn * L, D_MODEL)
        for lst, s in zip(prompt_states, (
                kp.reshape(bn, WINDOW, N_KV, HEAD_DIM), vp.reshape(bn, WINDOW, N_KV, HEAD_DIM),
                srp.reshape(bn, SSM_GROUPS, SSM_STATE), sip.reshape(bn, SSM_GROUPS, SSM_STATE),
                cvp, lrp.reshape(bn, LRU_WIDTH))):
            lst.append(s)

        ck = cache_k[l].reshape(nb, -1, KV_WIDTH)
        cv = cache_v[l].reshape(nb, -1, KV_WIDTH)
        conv0 = jnp.transpose(state_conv[l], (1, 0, 2)).reshape((CONV_WIDTH - 1) * nb, LRU_WIDTH)
        xs, ks, vs, srs, sis, cvs, lrs = _mixer_sample(
            xs, ck, cv, state_ssm_re[l].reshape(nb, SSM_FLAT), state_ssm_im[l].reshape(nb, SSM_FLAT),
            conv0, state_lru[l], p, nb, ns)
        xp, xs = _ffn(xp, xs, *f2, gfin, is_last)
        unflip = lambda m, w: jnp.transpose(m.reshape(-1, nb, w), (1, 0, 2))
        for lst, s in zip(sample_states, (
                unflip(ks, KV_WIDTH).reshape(nb, ns, N_KV, HEAD_DIM),
                unflip(vs, KV_WIDTH).reshape(nb, ns, N_KV, HEAD_DIM),
                srs.reshape(nb, SSM_GROUPS, SSM_STATE), sis.reshape(nb, SSM_GROUPS, SSM_STATE),
                unflip(cvs, LRU_WIDTH), lrs)):
            lst.append(s)

    y_prompt = xp.reshape(bn, L, D_MODEL)
    y_sample = jnp.transpose(xs.reshape(ns, nb, D_MODEL), (1, 0, 2))
    return (y_prompt, y_sample, *[jnp.stack(c, axis=0) for c in prompt_states],
            *[jnp.stack(c, axis=0) for c in sample_states])
```

```python
import functools
import math

import jax
import jax.numpy as jnp
from jax import lax
from jax.experimental import pallas as pl
from jax.experimental.pallas import tpu as pltpu

F32 = jnp.float32
BF16 = jnp.bfloat16

D_MODEL = 1024
D_FF = 2816
DEPTH = 2
CHUNK = 64
SSM_WIDTH = 256
SSM_GROUP = 16
SSM_GROUPS = 16
SSM_STATE = 64
SSM_FLAT = SSM_GROUPS * SSM_STATE
HEAD_DIM = 64
ATT_WIDTH = 512
N_HEADS = 8
N_KV = 2
Q_PER_KV = 4
KV_WIDTH = 128
WINDOW = 128
LRU_WIDTH = 256
LRU_BLOCKS = 4
LRU_BLOCK = 64
CONV_WIDTH = 4
LRU_C = 8.0
OFF_Q = SSM_WIDTH
OFF_K = OFF_Q + ATT_WIDTH
OFF_V = OFF_K + KV_WIDTH
OFF_LX = OFF_V + KV_WIDTH
OFF_LG = OFF_LX + LRU_WIDTH
IN_WIDTH = OFF_LG + LRU_WIDTH
ATT_SCALE = HEAD_DIM ** -0.5
assert math.frexp(ATT_SCALE)[0] == 0.5, "the prompt mixer pre-scales q, exact only for a power-of-two scale"
EPS = 1e-6
NEG_INF = -1e30

SUBLANES = 8
LANES = 128
FFN_ROWS = 512
FFN_COLS = 256
STAGE_SLOTS = 4
STAGE_IN_ROWS = 128
STAGE_OUT_ROWS = 256
MIX_ROWS = 512
PROJ_COLS = 256
SEGS = SUBLANES
SEG_LEN = MIX_ROWS // SEGS
VMEM_LIMIT = 56 * 1024 * 1024


def _rms(x, g):
    return x * lax.rsqrt(jnp.mean(x * x, axis=-1, keepdims=True) + EPS) * g


def _dot(a, b):
    return jnp.dot(a, b, preferred_element_type=F32)


def _dot_row_halves(a, b):
    half = a.shape[0] // 2
    return jnp.concatenate([_dot(a[:half], b), _dot(a[half:], b)], axis=0)


def _cmul(ar, ai, br, bi):
    return ar * br - ai * bi, ar * bi + ai * br


def _put(ref, val):
    for c in range(ref.shape[0]):
        ref[c] = val[:, c * LANES:(c + 1) * LANES]


def _get_strided(ref, start, n, stride):
    return jnp.concatenate(
        [ref[c, pl.ds(start, n, stride=stride), :] for c in range(ref.shape[0])], axis=1)


def _weight_jobs(layer, wg_hbm, wu_hbm, wd_hbm, wg_v, wu_v, wd_v, stage_in, stage_out, sem):
    jobs = []
    used = [0, 0]

    def add(src, dst, stage, kind, r0, n):
        slot = used[kind] % STAGE_SLOTS
        used[kind] += 1
        copy = pltpu.make_async_copy(src.at[layer, pl.ds(r0, n), :], stage.at[slot], sem.at[kind, slot])
        jobs.append((copy, stage.at[slot], dst, r0, n))

    for r0 in range(0, D_MODEL, STAGE_IN_ROWS):
        add(wg_hbm, wg_v, stage_in, 0, r0, STAGE_IN_ROWS)
        add(wu_hbm, wu_v, stage_in, 0, r0, STAGE_IN_ROWS)
    for r0 in range(0, D_FF, STAGE_OUT_ROWS):
        add(wd_hbm, wd_v, stage_out, 1, r0, STAGE_OUT_ROWS)
    return jobs


def _ffn_kernel(xp_ref, xs_ref, g_ref, wg_hbm, wu_hbm, wd_hbm, gf_ref, op_ref, os_ref,
                wg_ref, wu_ref, wd_ref, stage_in, stage_out, sem, *, layer, final_norm, prompt_tiles):
    i = pl.program_id(0)

    @pl.when(i == 0)
    def _():
        jobs = _weight_jobs(layer, wg_hbm, wu_hbm, wd_hbm, wg_ref, wu_ref, wd_ref, stage_in, stage_out, sem)
        ahead = STAGE_SLOTS - 1
        for k in range(min(ahead, len(jobs))):
            jobs[k][0].start()
        for k, (copy, staged, dst, r0, n) in enumerate(jobs):
            if k + ahead < len(jobs):
                jobs[k + ahead][0].start()
            copy.wait()
            dst[r0:r0 + n, :] = staged[...].astype(BF16)

    x = jnp.where(i < prompt_tiles, xp_ref[...], xs_ref[...])
    h = _rms(x, g_ref[...]).astype(BF16)
    acc = jnp.zeros(x.shape, F32)
    for c in range(D_FF // FFN_COLS):
        sl = slice(c * FFN_COLS, (c + 1) * FFN_COLS)
        g = _dot(h, wg_ref[:, sl])
        u = _dot(h, wu_ref[:, sl])
        a = (g * jax.nn.sigmoid(g) * u).astype(BF16)
        acc = acc + _dot(a, wd_ref[sl, :])
    y = x + 0.5 * acc
    if final_norm:
        y = _rms(y, gf_ref[...])

    @pl.when(i < prompt_tiles)
    def _():
        op_ref[...] = y

    @pl.when(i >= prompt_tiles)
    def _():
        os_ref[...] = y


def _ffn(xp, xs, layer, g, wg, wu, wd, gf, final_norm):
    prompt_tiles = xp.shape[0] // FFN_ROWS
    assert xs.shape[0] == FFN_ROWS
    const = lambda i: (0, 0)
    prompt_map = lambda i: (jnp.minimum(i, prompt_tiles - 1), 0)
    hbm = pl.BlockSpec(memory_space=pl.ANY)
    return pl.pallas_call(
        functools.partial(_ffn_kernel, layer=layer, final_norm=final_norm, prompt_tiles=prompt_tiles),
        grid=(prompt_tiles + 1,),
        in_specs=[
            pl.BlockSpec((FFN_ROWS, D_MODEL), prompt_map),
            pl.BlockSpec((FFN_ROWS, D_MODEL), const),
            pl.BlockSpec((1, D_MODEL), const),
            hbm, hbm, hbm,
            pl.BlockSpec((1, D_MODEL), const),
        ],
        out_specs=[pl.BlockSpec((FFN_ROWS, D_MODEL), prompt_map), pl.BlockSpec((FFN_ROWS, D_MODEL), const)],
        out_shape=[jax.ShapeDtypeStruct(xp.shape, F32), jax.ShapeDtypeStruct(xs.shape, F32)],
        scratch_shapes=[
            pltpu.VMEM((D_MODEL, D_FF), BF16), pltpu.VMEM((D_MODEL, D_FF), BF16), pltpu.VMEM((D_FF, D_MODEL), BF16),
            pltpu.VMEM((STAGE_SLOTS, STAGE_IN_ROWS, D_FF), F32),
            pltpu.VMEM((STAGE_SLOTS, STAGE_OUT_ROWS, D_MODEL), F32),
            pltpu.SemaphoreType.DMA((2, STAGE_SLOTS)),
        ],
        compiler_params=pltpu.CompilerParams(
            dimension_semantics=("arbitrary",), vmem_limit_bytes=VMEM_LIMIT),
        name="ffn",
    )(xp, xs, g, wg, wu, wd, gf)


def _gelu(x):
    z = math.sqrt(2.0 / math.pi) * (x + 0.044715 * (x * x * x))
    return x * jax.nn.sigmoid(2.0 * z)


def _log_sigmoid(x):
    return jnp.minimum(x, 0.0) - jnp.log1p(jnp.exp(-jnp.abs(x)))


def _s5_post(y, u, d, wglu, g):
    z = _gelu(y + d * u)
    out = z * jax.nn.sigmoid(_dot_row_halves(z.astype(BF16), wglu))
    return _rms(out, g)


def _lru_gates(xc, wax, b_a, b_x, lam):
    ga = _dot(xc.astype(BF16), wax)
    r = jax.nn.sigmoid(ga[:, :LRU_WIDTH] + b_a)
    i = jax.nn.sigmoid(ga[:, LRU_WIDTH:] + b_x)
    log_a = LRU_C * r * _log_sigmoid(lam)
    a = jnp.exp(log_a)
    t = jnp.tanh(log_a)
    mult = jnp.sqrt(-2.0 * t / (1.0 - t))
    return a, mult * i * xc


def _sink_softmax(s, sink_col):
    m = jnp.maximum(jnp.max(s, axis=-1, keepdims=True), sink_col)
    e = jnp.exp(s - m)
    den = jnp.sum(e, axis=-1, keepdims=True) + jnp.exp(sink_col - m)
    return e / den


def _mix_out(x, ys, ya, yl, gout, wout):
    ycat = jnp.concatenate([
        ys,
        _rms(ya, gout[:, SSM_WIDTH:SSM_WIDTH + ATT_WIDTH]),
        _rms(yl, gout[:, SSM_WIDTH + ATT_WIDTH:]),
    ], axis=1).astype(BF16)
    return x + _dot(ycat, wout)


def _mixer_prompt_kernel(
        x_ref, xnext_ref, gmix_ref, win_ref, lbre_ref, lbim_ref, bbig_ref, cbig_ref, d_ref, wglu_ref, sink_ref,
        convw_ref, convb_ref, wax_ref, ba_ref, bx_ref, lam_ref, gout_ref, wout_ref, unperm_ref,
        o_ref, kout_ref, vout_ref, sre_ref, sim_ref, convout_ref, lruout_ref,
        st_re, st_im, st_lru, u_buf, up_buf, xs_buf, hs_buf, xl_buf, a_buf, b_buf, hl_buf, lb_buf,
        qz_buf, k_buf, vt_buf, yt_buf, proj_buf, hnext_buf):
    j = pl.program_id(1)
    last = pl.num_programs(1) - 1
    T = MIX_ROWS

    @pl.when(j == 0)
    def _():
        st_re[...] = jnp.zeros_like(st_re)
        st_im[...] = jnp.zeros_like(st_im)
        st_lru[...] = jnp.zeros_like(st_lru)
        xl_buf[0:SUBLANES, :] = jnp.zeros((SUBLANES, LRU_WIDTH), F32)
        k_buf[0:WINDOW, :] = jnp.zeros((WINDOW, KV_WIDTH), BF16)
        vt_buf[0] = jnp.zeros((KV_WIDTH, WINDOW), BF16)

    @pl.when(jnp.logical_and(pl.program_id(0) == 0, j == 0))
    def _():
        proj_buf[...] = _dot(_rms(x_ref[0], gmix_ref[...]).astype(BF16), win_ref[...])

    x = x_ref[0]
    proj = proj_buf[...]
    hnext_buf[...] = _rms(xnext_ref[0], gmix_ref[...]).astype(BF16)
    pending = list(range(0, IN_WIDTH, PROJ_COLS))

    def project_next_block():
        c0 = pending.pop(0)
        proj_buf[:, c0:c0 + PROJ_COLS] = _dot(hnext_buf[...], win_ref[:, c0:c0 + PROJ_COLS])

    u = proj[:, :OFF_Q]
    k = proj[:, OFF_K:OFF_V]
    v = proj[:, OFF_V:OFF_LX]
    xl = proj[:, OFF_LX:OFF_LG]
    gl = proj[:, OFF_LG:]
    gout = gout_ref[...]

    _put(u_buf, u)
    for i in range(SEG_LEN):
        up_buf[i * SEGS:(i + 1) * SEGS, :] = _get_strided(u_buf, i, SEGS, SEG_LEN)
    up = up_buf[...]
    xs_buf[...] = _dot(up.astype(BF16), bbig_ref[...])
    lr = lbre_ref[...]
    li = lbim_ref[...]
    lb_buf[0] = jnp.broadcast_to(lr, (SEGS, SSM_FLAT))
    lb_buf[1] = jnp.broadcast_to(li, (SEGS, SSM_FLAT))

    def s5_two_steps(ii, carry, store):
        hr, hi = carry
        r0 = ii * 2 * SEGS
        lrb = lb_buf[0]
        lib = lb_buf[1]
        new_r, new_i = [], []
        for half in range(2):
            xr = xs_buf[pl.ds(r0 + half * SEGS, SEGS), 0:SSM_FLAT]
            xi = xs_buf[pl.ds(r0 + half * SEGS, SEGS), SSM_FLAT:2 * SSM_FLAT]
            hr, hi = lrb * hr - lib * hi + xr, lrb * hi + lib * hr + xi
            new_r.append(hr)
            new_i.append(hi)
        if store:
            hs_buf[pl.ds(r0, 2 * SEGS), 0:SSM_FLAT] = jnp.concatenate(new_r, axis=0).astype(BF16)
            hs_buf[pl.ds(r0, 2 * SEGS), SSM_FLAT:2 * SSM_FLAT] = jnp.concatenate(new_i, axis=0).astype(BF16)
        return hr, hi

    zero_seg = jnp.zeros((SEGS, SSM_FLAT), F32)
    pairs = SEG_LEN // 2
    state = (zero_seg, zero_seg)
    for ii in range(pairs):
        if ii in (0, pairs // 2):
            project_next_block()
        state = s5_two_steps(ii, state, store=False)
    loc_r, loc_i = state
    pr, pi = lr, li
    for _ in range(int(math.log2(SEG_LEN))):
        pr, pi = _cmul(pr, pi, pr, pi)
    gr, gi = st_re[...], st_im[...]
    rows_r, rows_i = [], []
    for s in range(SEGS):
        rows_r.append(gr)
        rows_i.append(gi)
        ar, ai = _cmul(pr, pi, gr, gi)
        gr = ar + loc_r[s:s + 1, :]
        gi = ai + loc_i[s:s + 1, :]
    st_re[...] = gr
    st_im[...] = gi
    state = (jnp.concatenate(rows_r, axis=0), jnp.concatenate(rows_i, axis=0))
    for ii in range(pairs):
        if ii in (0, pairs // 2):
            project_next_block()
        state = s5_two_steps(ii, state, store=True)
    y = _dot_row_halves(hs_buf[...], cbig_ref[...])
    xl_buf[SUBLANES:SUBLANES + T, :] = xl
    cw = convw_ref[...]
    xc = convb_ref[...] + cw[0:1, :] * xl_buf[SUBLANES - 3:SUBLANES - 3 + T, :]
    for t in range(1, CONV_WIDTH):
        xc = xc + cw[t:t + 1, :] * xl_buf[SUBLANES - 3 + t:SUBLANES - 3 + t + T, :]
    tail = xl_buf[T:T + SUBLANES, :]
    xl_buf[0:SUBLANES, :] = tail
    project_next_block()
    a, b = _lru_gates(xc, wax_ref[...], ba_ref[...], bx_ref[...], lam_ref[...])
    a3 = a.reshape(T // SUBLANES, SUBLANES, LRU_WIDTH)
    b3 = b.reshape(T // SUBLANES, SUBLANES, LRU_WIDTH)
    slab_row = lax.broadcasted_iota(jnp.int32, a3.shape, 1)
    shift = 1
    while shift < SUBLANES:
        keep = slab_row >= shift
        b3 = jnp.where(keep, a3 * pltpu.roll(b3, shift, axis=1) + b3, b3)
        a3 = jnp.where(keep, a3 * pltpu.roll(a3, shift, axis=1), a3)
        shift *= 2
    a_buf[...] = a3.reshape(T, LRU_WIDTH)
    b_buf[...] = b3.reshape(T, LRU_WIDTH)
    project_next_block()

    def lru_slab(i, carry):
        r0 = pl.multiple_of(i * SUBLANES, SUBLANES)
        hh = a_buf[pl.ds(r0, SUBLANES), :] * carry + b_buf[pl.ds(r0, SUBLANES), :]
        hl_buf[pl.ds(r0, SUBLANES), :] = hh
        return jnp.broadcast_to(hh[SUBLANES - 1:SUBLANES, :], (SUBLANES, LRU_WIDTH))

    lru_carry = lax.fori_loop(0, T // SUBLANES, lru_slab,
                              jnp.broadcast_to(st_lru[...], (SUBLANES, LRU_WIDTH)), unroll=True)
    gl_state = lru_carry[0:1, :]
    st_lru[...] = gl_state
    yl = _gelu(gl) * hl_buf[...]

    ys_regrouped = _s5_post(y, up, d_ref[...], wglu_ref[...], gout[:, :SSM_WIDTH]).astype(BF16)
    ys = _dot_row_halves(unperm_ref[...], ys_regrouped)

    lo = lax.broadcasted_iota(jnp.int32, (T, LANES), 1) < HEAD_DIM
    for jp in range(N_HEADS // 2):
        pair = proj[:, OFF_Q + jp * LANES:OFF_Q + (jp + 1) * LANES] * ATT_SCALE
        swapped = pltpu.roll(pair, HEAD_DIM, axis=1)
        if jp < N_HEADS // 4:
            even, odd = jnp.where(lo, pair, 0.0), jnp.where(lo, swapped, 0.0)
        else:
            even, odd = jnp.where(lo, 0.0, swapped), jnp.where(lo, 0.0, pair)
        qz_buf[2 * jp] = even.astype(BF16)
        qz_buf[2 * jp + 1] = odd.astype(BF16)
    k_buf[WINDOW:WINDOW + T, :] = k.astype(BF16)
    vt = v.T.astype(BF16)
    for i in range(T // WINDOW):
        vt_buf[i + 1] = vt[:, i * WINDOW:(i + 1) * WINDOW]

    span = 2 * CHUNK
    group = Q_PER_KV * span
    krow = lax.broadcasted_iota(jnp.int32, (2 * span, group), 0)
    qcol = lax.broadcasted_iota(jnp.int32, (2 * span, group), 1) % span
    band = jnp.logical_or(jnp.logical_and(qcol < CHUNK, krow < WINDOW + CHUNK),
                          jnp.logical_and(qcol >= CHUNK, krow >= CHUNK))
    for m in range(T // span):
        valid = band if m > 0 else jnp.logical_and(band, jnp.logical_or(krow >= WINDOW, j > 0))
        top, bot = (CHUNK if m > 0 else WINDOW), WINDOW + CHUNK
        kwin = k_buf[m * span:(m + 2) * span, :]
        vtwin = jnp.concatenate([vt_buf[m], vt_buf[m + 1]], axis=1)
        for kvh in range(N_KV):
            heads = range(kvh * Q_PER_KV, (kvh + 1) * Q_PER_KV)
            qz = jnp.concatenate([qz_buf[h, m * span:(m + 1) * span, :] for h in heads], axis=0)
            s = lax.dot_general(kwin, qz, (((1,), (1,)), ((), ())), preferred_element_type=F32)
            s = jnp.concatenate([jnp.where(valid[:top], s[:top], NEG_INF), s[top:bot],
                                 jnp.where(valid[bot:], s[bot:], NEG_INF)], axis=0)
            sink_row = jnp.concatenate([jnp.full((1, span), sink_ref[h], F32) for h in heads], axis=1)
            mx = jnp.maximum(jnp.max(s, axis=0, keepdims=True), sink_row)
            e = jnp.exp(s - mx)
            den = jnp.sum(e, axis=0, keepdims=True) + jnp.exp(sink_row - mx)
            o = _dot(vtwin[kvh * HEAD_DIM:(kvh + 1) * HEAD_DIM, :], e.astype(BF16)) * (1.0 / den)
            for g, h in enumerate(heads):
                yt_buf[h * HEAD_DIM:(h + 1) * HEAD_DIM, m * span:(m + 1) * span] = o[:, g * span:(g + 1) * span]
    k_buf[0:WINDOW, :] = k_buf[T:T + WINDOW, :]
    vt_buf[0] = vt_buf[T // WINDOW]

    o_ref[0] = _mix_out(x, ys, yt_buf[...].T, yl, gout, wout_ref[...])
    while pending:
        project_next_block()

    @pl.when(j == last)
    def _():
        kout_ref[0] = k[T - WINDOW:, :]
        vout_ref[0] = v[T - WINDOW:, :]
        sre_ref[0] = gr
        sim_ref[0] = gi
        convout_ref[0] = tail[SUBLANES - (CONV_WIDTH - 1):, :]
        lruout_ref[0] = gl_state


def _mixer_prompt(x, p):
    bn, L, _ = x.shape
    T = MIX_ROWS
    const2 = lambda b, j: (0, 0)
    vspec = lambda shape: pl.BlockSpec(shape, const2)
    per_b = lambda shape: pl.BlockSpec((1,) + shape, lambda b, j: (b, 0, 0))
    tiles = L // T

    def next_tile(b, j):
        n = jnp.minimum(b * tiles + j + 1, bn * tiles - 1)
        return (n // tiles, n % tiles, 0)

    in_specs = [
        pl.BlockSpec((1, T, D_MODEL), lambda b, j: (b, j, 0)),
        pl.BlockSpec((1, T, D_MODEL), next_tile),
        vspec((1, D_MODEL)), vspec((D_MODEL, IN_WIDTH)),
        vspec((1, SSM_FLAT)), vspec((1, SSM_FLAT)),
        vspec((SSM_WIDTH, 2 * SSM_FLAT)), vspec((2 * SSM_FLAT, SSM_WIDTH)),
        vspec((1, SSM_WIDTH)), vspec((SSM_WIDTH, SSM_WIDTH)),
        pl.BlockSpec(memory_space=pltpu.SMEM),
        vspec((CONV_WIDTH, LRU_WIDTH)), vspec((1, LRU_WIDTH)),
        vspec((LRU_WIDTH, 2 * LRU_WIDTH)), vspec((1, LRU_WIDTH)), vspec((1, LRU_WIDTH)), vspec((1, LRU_WIDTH)),
        vspec((1, D_MODEL)), vspec((D_MODEL, D_MODEL)), vspec((T, T)),
    ]
    out_specs = [
        pl.BlockSpec((1, T, D_MODEL), lambda b, j: (b, j, 0)),
        per_b((WINDOW, KV_WIDTH)), per_b((WINDOW, KV_WIDTH)),
        per_b((1, SSM_FLAT)), per_b((1, SSM_FLAT)),
        per_b((CONV_WIDTH - 1, LRU_WIDTH)), per_b((1, LRU_WIDTH)),
    ]
    out_shape = [
        jax.ShapeDtypeStruct((bn, L, D_MODEL), F32),
        jax.ShapeDtypeStruct((bn, WINDOW, KV_WIDTH), F32), jax.ShapeDtypeStruct((bn, WINDOW, KV_WIDTH), F32),
        jax.ShapeDtypeStruct((bn, 1, SSM_FLAT), F32), jax.ShapeDtypeStruct((bn, 1, SSM_FLAT), F32),
        jax.ShapeDtypeStruct((bn, CONV_WIDTH - 1, LRU_WIDTH), F32), jax.ShapeDtypeStruct((bn, 1, LRU_WIDTH), F32),
    ]
    planes = lambda width: pltpu.VMEM((width // LANES, T, LANES), F32)
    scratch = [
        pltpu.VMEM((1, SSM_FLAT), F32), pltpu.VMEM((1, SSM_FLAT), F32), pltpu.VMEM((1, LRU_WIDTH), F32),
        planes(SSM_WIDTH), pltpu.VMEM((T, SSM_WIDTH), F32),
        pltpu.VMEM((T, 2 * SSM_FLAT), F32), pltpu.VMEM((T, 2 * SSM_FLAT), BF16),
        pltpu.VMEM((T + SUBLANES, LRU_WIDTH), F32),
        pltpu.VMEM((T, LRU_WIDTH), F32), pltpu.VMEM((T, LRU_WIDTH), F32), pltpu.VMEM((T, LRU_WIDTH), F32),
        pltpu.VMEM((2, SEGS, SSM_FLAT), F32),
        pltpu.VMEM((N_HEADS, T, LANES), BF16),
        pltpu.VMEM((T + WINDOW, KV_WIDTH), BF16),
        pltpu.VMEM((T // WINDOW + 1, KV_WIDTH, WINDOW), BF16),
        pltpu.VMEM((ATT_WIDTH, T), F32),
        pltpu.VMEM((T, IN_WIDTH), F32), pltpu.VMEM((T, D_MODEL), BF16),
    ]
    return pl.pallas_call(
        _mixer_prompt_kernel,
        grid=(bn, L // T),
        in_specs=in_specs, out_specs=out_specs, out_shape=out_shape, scratch_shapes=scratch,
        compiler_params=pltpu.CompilerParams(
            dimension_semantics=("arbitrary", "arbitrary"), vmem_limit_bytes=VMEM_LIMIT),
        name="mixer_prompt",
    )(x, x, p["gmix"], p["win"], p["lb_re"], p["lb_im"], p["bbig"], p["cbig"], p["d"], p["wglu"], p["sink"],
      p["convw"], p["convb"], p["wax"], p["b_a"], p["b_x"], p["lam"], p["gout"], p["wout"], _unpermute_matrix())


def _unpermute_matrix():
    token = jnp.arange(MIX_ROWS)
    source = (token % SEG_LEN) * SEGS + token // SEG_LEN
    return (source[:, None] == jnp.arange(MIX_ROWS)[None, :]).astype(BF16)


def _mixer_sample_kernel(
        x_ref, ck_ref, cv_ref, s0re_ref, s0im_ref, conv0_ref, lru0_ref,
        gmix_ref, win_ref, lbre_ref, lbim_ref, bbig_ref, cbig_ref, d_ref, wglu_ref, sink_ref,
        convw_ref, convb_ref, wax_ref, ba_ref, bx_ref, lam_ref, gout_ref, wout_ref, to_stream_ref, to_time_ref,
        o_ref, kout_ref, vout_ref, sre_ref, sim_ref, convout_ref, lruout_ref,
        xs_buf, xp_buf, a_buf, b_buf, hl_buf, qz_buf, kn_buf, vn_buf, ya_buf, *, nb, ns):
    rows = nb * ns
    x = x_ref[...]
    h = _rms(x, gmix_ref[...]).astype(BF16)
    proj = _dot(h, win_ref[...])
    u = proj[:, :OFF_Q]
    k = proj[:, OFF_K:OFF_V]
    v = proj[:, OFF_V:OFF_LX]
    xl = proj[:, OFF_LX:OFF_LG]
    gl = proj[:, OFF_LG:]
    gout = gout_ref[...]
    kout_ref[...] = k
    vout_ref[...] = v

    xs_buf[...] = _dot(u.astype(BF16), bbig_ref[...])
    lr = lbre_ref[...]
    li = lbim_ref[...]
    hr = s0re_ref[...]
    hi = s0im_ref[...]
    for t in range(ns):
        sl = slice(t * nb, (t + 1) * nb)
        nr = lr * hr - li * hi + xs_buf[sl, 0:SSM_FLAT]
        ni = lr * hi + li * hr + xs_buf[sl, SSM_FLAT:2 * SSM_FLAT]
        hr, hi = nr, ni
        xs_buf[sl, 0:SSM_FLAT] = hr
        xs_buf[sl, SSM_FLAT:2 * SSM_FLAT] = hi
    sre_ref[...] = hr
    sim_ref[...] = hi
    y = _dot_row_halves(xs_buf[...].astype(BF16), cbig_ref[...])
    ys = _s5_post(y, u, d_ref[...], wglu_ref[...], gout[:, :SSM_WIDTH])

    npre = (CONV_WIDTH - 1) * nb
    xp_buf[0:npre, :] = conv0_ref[...]
    xp_buf[npre:npre + rows, :] = xl
    cw = convw_ref[...]
    xc = convb_ref[...] + cw[0:1, :] * xp_buf[0:rows, :]
    for t in range(1, CONV_WIDTH):
        xc = xc + cw[t:t + 1, :] * xp_buf[t * nb:t * nb + rows, :]
    convout_ref[...] = xp_buf[rows:rows + npre, :]
    a, b = _lru_gates(xc, wax_ref[...], ba_ref[...], bx_ref[...], lam_ref[...])
    a_buf[...] = a
    b_buf[...] = b
    hh = lru0_ref[...]
    for t in range(ns):
        sl = slice(t * nb, (t + 1) * nb)
        hh = a_buf[sl, :] * hh + b_buf[sl, :]
        hl_buf[sl, :] = hh
    lruout_ref[...] = hh
    yl = _gelu(gl) * hl_buf[...]

    qkv = jnp.concatenate([proj[:, OFF_Q:OFF_K] * ATT_SCALE, k, v], axis=1).astype(BF16)
    qkv = _dot(to_stream_ref[...], qkv)
    lo = lax.broadcasted_iota(jnp.int32, (rows, LANES), 1) < HEAD_DIM
    for jp in range(N_HEADS // 2):
        pair = qkv[:, jp * LANES:(jp + 1) * LANES]
        swapped = pltpu.roll(pair, HEAD_DIM, axis=1)
        if jp < N_HEADS // 4:
            even, odd = jnp.where(lo, pair, 0.0), jnp.where(lo, swapped, 0.0)
        else:
            even, odd = jnp.where(lo, 0.0, swapped), jnp.where(lo, 0.0, pair)
        qz_buf[2 * jp] = even.astype(BF16)
        qz_buf[2 * jp + 1] = odd.astype(BF16)
    kn_buf[...] = qkv[:, ATT_WIDTH:ATT_WIDTH + KV_WIDTH].astype(BF16)
    vn_buf[...] = qkv[:, ATT_WIDTH + KV_WIDTH:].astype(BF16)
    sink_col = jnp.concatenate([jnp.full((ns, 1), sink_ref[h], F32) for h in range(N_HEADS)], axis=0)
    lo_q = lax.broadcasted_iota(jnp.int32, (ns, LANES), 1) < HEAD_DIM

    def attn_stream(bi, _):
        r0 = pl.multiple_of(bi * ns, ns)
        qz = jnp.concatenate([qz_buf[h, pl.ds(r0, ns), :] for h in range(N_HEADS)], axis=0)
        kk = jnp.concatenate([ck_ref[bi].astype(BF16), kn_buf[pl.ds(r0, ns), :]], axis=0)
        vv = jnp.concatenate([cv_ref[bi].astype(BF16), vn_buf[pl.ds(r0, ns), :]], axis=0)
        s = lax.dot_general(qz, kk, (((1,), (1,)), ((), ())), preferred_element_type=F32)
        o = _dot(_sink_softmax(s, sink_col).astype(BF16), vv)
        pieces = []
        for jp in range(N_HEADS // 2):
            oa = o[2 * jp * ns:(2 * jp + 1) * ns, :]
            ob = o[(2 * jp + 1) * ns:(2 * jp + 2) * ns, :]
            if jp < N_HEADS // 4:
                pieces.append(jnp.where(lo_q, oa, pltpu.roll(ob, HEAD_DIM, axis=1)))
            else:
                pieces.append(jnp.where(lo_q, pltpu.roll(oa, HEAD_DIM, axis=1), ob))
        ya_buf[pl.ds(r0, ns), :] = jnp.concatenate(pieces, axis=1)
        return 0

    lax.fori_loop(0, nb, attn_stream, 0, unroll=2)
    ya = _rms(ya_buf[...], gout[:, SSM_WIDTH:SSM_WIDTH + ATT_WIDTH]).astype(BF16)
    ycat = jnp.concatenate([ys, _dot(to_time_ref[...], ya), _rms(yl, gout[:, SSM_WIDTH + ATT_WIDTH:])],
                           axis=1).astype(BF16)
    o_ref[...] = x + _dot(ycat, wout_ref[...])


def _mixer_sample(x, ck, cv, s0re, s0im, conv0, lru0, p, nb, ns):
    rows = nb * ns
    vm = pl.BlockSpec(memory_space=pltpu.VMEM)
    in_specs = [vm] * 15 + [pl.BlockSpec(memory_space=pltpu.SMEM)] + [vm] * 10
    time_row = jnp.arange(rows)
    stream_row = (time_row % nb) * ns + time_row // nb
    to_stream = (jnp.arange(rows)[:, None] == stream_row[None, :]).astype(BF16)
    out_shape = [
        jax.ShapeDtypeStruct((rows, D_MODEL), F32),
        jax.ShapeDtypeStruct((rows, KV_WIDTH), F32), jax.ShapeDtypeStruct((rows, KV_WIDTH), F32),
        jax.ShapeDtypeStruct((nb, SSM_FLAT), F32), jax.ShapeDtypeStruct((nb, SSM_FLAT), F32),
        jax.ShapeDtypeStruct(((CONV_WIDTH - 1) * nb, LRU_WIDTH), F32), jax.ShapeDtypeStruct((nb, LRU_WIDTH), F32),
    ]
    scratch = [
        pltpu.VMEM((rows, 2 * SSM_FLAT), F32),
        pltpu.VMEM((rows + (CONV_WIDTH - 1) * nb, LRU_WIDTH), F32),
        pltpu.VMEM((rows, LRU_WIDTH), F32), pltpu.VMEM((rows, LRU_WIDTH), F32), pltpu.VMEM((rows, LRU_WIDTH), F32),
        pltpu.VMEM((N_HEADS, rows, LANES), BF16),
        pltpu.VMEM((rows, KV_WIDTH), BF16), pltpu.VMEM((rows, KV_WIDTH), BF16),
        pltpu.VMEM((rows, ATT_WIDTH), F32),
    ]
    return pl.pallas_call(
        functools.partial(_mixer_sample_kernel, nb=nb, ns=ns),
        in_specs=in_specs, out_specs=[vm] * 7, out_shape=out_shape, scratch_shapes=scratch,
        compiler_params=pltpu.CompilerParams(vmem_limit_bytes=VMEM_LIMIT),
        name="mixer_sample",
    )(x, ck, cv, s0re, s0im, conv0, lru0,
      p["gmix"], p["win"], p["lb_re"], p["lb_im"], p["bbig"], p["cbig"], p["d"], p["wglu"], p["sink"],
      p["convw"], p["convb"], p["wax"], p["b_a"], p["b_x"], p["lam"], p["gout"], p["wout"], to_stream, to_stream.T)


def _block_diag(w):
    n, i, o = w.shape
    eye = jnp.eye(n, dtype=w.dtype)
    return jnp.einsum("nm,nio->nimo", eye, w).reshape(n * i, n * o)


def _layer_params(l, mix_norm, w_in, ssm_a_re, ssm_a_im, ssm_log_dt, ssm_b_re, ssm_b_im, ssm_c_re, ssm_c_im,
                  ssm_d, ssm_w_glu, attn_sink, conv_w, conv_b, lru_w_a, lru_b_a, lru_w_x, lru_b_x, lru_lambda,
                  out_norm, w_out):
    a_re = ssm_a_re[l]
    a_im = ssm_a_im[l]
    dt = jnp.exp(ssm_log_dt[l])[:, None]
    mag = jnp.exp(a_re * dt)
    lb_re = mag * jnp.cos(a_im * dt)
    lb_im = mag * jnp.sin(a_im * dt)
    den = a_re * a_re + a_im * a_im
    nr = lb_re - 1.0
    k_re = (nr * a_re + lb_im * a_im) / den
    k_im = (lb_im * a_re - nr * a_im) / den
    b_re = ssm_b_re[l]
    b_im = ssm_b_im[l]
    bb_re = k_re[..., None] * b_re - k_im[..., None] * b_im
    bb_im = k_re[..., None] * b_im + k_im[..., None] * b_re
    to_gh_p = lambda m: jnp.transpose(m, (0, 2, 1))
    bbig = jnp.concatenate([_block_diag(to_gh_p(bb_re)), _block_diag(to_gh_p(bb_im))], axis=1)
    to_gp_h = lambda m: jnp.transpose(m, (0, 2, 1))
    cbig = jnp.concatenate([_block_diag(to_gp_h(ssm_c_re[l])), -_block_diag(to_gp_h(ssm_c_im[l]))], axis=0)
    row = lambda vec: vec.reshape(1, -1)
    return {
        "gmix": row(mix_norm[l]), "win": w_in[l].astype(BF16),
        "lb_re": row(lb_re), "lb_im": row(lb_im),
        "bbig": bbig.astype(BF16), "cbig": cbig.astype(BF16),
        "d": row(ssm_d[l]), "wglu": ssm_w_glu[l].astype(BF16), "sink": attn_sink[l],
        "convw": conv_w[l], "convb": row(conv_b[l]),
        "wax": jnp.concatenate([_block_diag(lru_w_a[l]), _block_diag(lru_w_x[l])], axis=1).astype(BF16),
        "b_a": row(lru_b_a[l]), "b_x": row(lru_b_x[l]), "lam": row(lru_lambda[l]),
        "gout": row(out_norm[l]), "wout": w_out[l].astype(BF16),
    }


def kernel(x_prompt, x_sample, cache_k, cache_v, state_ssm_re, state_ssm_im, state_conv, state_lru,
           ffn1_norm, ffn1_w_gate, ffn1_w_up, ffn1_w_down, mix_norm, w_in,
           ssm_a_re, ssm_a_im, ssm_log_dt, ssm_b_re, ssm_b_im, ssm_c_re, ssm_c_im, ssm_d, ssm_w_glu,
           attn_sink, conv_w, conv_b, lru_w_a, lru_b_a, lru_w_x, lru_b_x, lru_lambda,
           out_norm, w_out, ffn2_norm, ffn2_w_gate, ffn2_w_up, ffn2_w_down, final_norm):
    bn, L, _ = x_prompt.shape
    nb, ns, _ = x_sample.shape
    row = lambda vec: vec.reshape(1, -1)
    gfin = row(final_norm)

    xp = x_prompt.reshape(bn * L, D_MODEL)
    xs = jnp.transpose(x_sample, (1, 0, 2)).reshape(ns * nb, D_MODEL)
    prompt_states = [[] for _ in range(6)]
    sample_states = [[] for _ in range(6)]
    for l in range(DEPTH):
        p = _layer_params(l, mix_norm, w_in, ssm_a_re, ssm_a_im, ssm_log_dt, ssm_b_re, ssm_b_im, ssm_c_re,
                          ssm_c_im, ssm_d, ssm_w_glu, attn_sink, conv_w, conv_b, lru_w_a, lru_b_a, lru_w_x,
                          lru_b_x, lru_lambda, out_norm, w_out)
        f1 = (l, row(ffn1_norm[l]), ffn1_w_gate, ffn1_w_up, ffn1_w_down)
        f2 = (l, row(ffn2_norm[l]), ffn2_w_gate, ffn2_w_up, ffn2_w_down)
        is_last = l == DEPTH - 1

        xp, xs = _ffn(xp, xs, *f1, gfin, False)
        xp3, kp, vp, srp, sip, cvp, lrp = _mixer_prompt(xp.reshape(bn, L, D_MODEL), p)
        xp = xp3.reshape(bn * L, D_MODEL)
        for lst, s in zip(prompt_states, (
                kp.reshape(bn, WINDOW, N_KV, HEAD_DIM), vp.reshape(bn, WINDOW, N_KV, HEAD_DIM),
                srp.reshape(bn, SSM_GROUPS, SSM_STATE), sip.reshape(bn, SSM_GROUPS, SSM_STATE),
                cvp, lrp.reshape(bn, LRU_WIDTH))):
            lst.append(s)

        ck = cache_k[l].reshape(nb, -1, KV_WIDTH)
        cv = cache_v[l].reshape(nb, -1, KV_WIDTH)
        conv0 = jnp.transpose(state_conv[l], (1, 0, 2)).reshape((CONV_WIDTH - 1) * nb, LRU_WIDTH)
        xs, ks, vs, srs, sis, cvs, lrs = _mixer_sample(
            xs, ck, cv, state_ssm_re[l].reshape(nb, SSM_FLAT), state_ssm_im[l].reshape(nb, SSM_FLAT),
            conv0, state_lru[l], p, nb, ns)
        xp, xs = _ffn(xp, xs, *f2, gfin, is_last)
        unflip = lambda m, w: jnp.transpose(m.reshape(-1, nb, w), (1, 0, 2))
        for lst, s in zip(sample_states, (
                unflip(ks, KV_WIDTH).reshape(nb, ns, N_KV, HEAD_DIM),
                unflip(vs, KV_WIDTH).reshape(nb, ns, N_KV, HEAD_DIM),
                srs.reshape(nb, SSM_GROUPS, SSM_STATE), sis.reshape(nb, SSM_GROUPS, SSM_STATE),
                unflip(cvs, LRU_WIDTH), lrs)):
            lst.append(s)

    y_prompt = xp.reshape(bn, L, D_MODEL)
    y_sample = jnp.transpose(xs.reshape(ns, nb, D_MODEL), (1, 0, 2))
    return (y_prompt, y_sample, *[jnp.stack(c, axis=0) for c in prompt_states],
            *[jnp.stack(c, axis=0) for c in sample_states])
```

```python
import functools
import math

import jax
import jax.numpy as jnp
from jax import lax
from jax.experimental import pallas as pl
from jax.experimental.pallas import tpu as pltpu

F32 = jnp.float32
BF16 = jnp.bfloat16

D_MODEL = 1024
D_FF = 2816
DEPTH = 2
CHUNK = 64
SSM_WIDTH = 256
SSM_GROUP = 16
SSM_GROUPS = 16
SSM_STATE = 64
SSM_FLAT = SSM_GROUPS * SSM_STATE
HEAD_DIM = 64
ATT_WIDTH = 512
N_HEADS = 8
N_KV = 2
Q_PER_KV = 4
KV_WIDTH = 128
WINDOW = 128
LRU_WIDTH = 256
LRU_BLOCKS = 4
LRU_BLOCK = 64
CONV_WIDTH = 4
LRU_C = 8.0
OFF_Q = SSM_WIDTH
OFF_K = OFF_Q + ATT_WIDTH
OFF_V = OFF_K + KV_WIDTH
OFF_LX = OFF_V + KV_WIDTH
OFF_LG = OFF_LX + LRU_WIDTH
IN_WIDTH = OFF_LG + LRU_WIDTH
ATT_SCALE = HEAD_DIM ** -0.5
assert math.frexp(ATT_SCALE)[0] == 0.5, "the prompt mixer pre-scales q, exact only for a power-of-two scale"
EPS = 1e-6
NEG_INF = -1e30

SUBLANES = 8
LANES = 128
FFN_ROWS = 512
FFN_COLS = 256
STAGE_SLOTS = 4
STAGE_IN_ROWS = 128
STAGE_OUT_ROWS = 256
MIX_ROWS = 512
PROJ_COLS = 256
SEGS = SUBLANES
SEG_LEN = MIX_ROWS // SEGS
VMEM_LIMIT = 56 * 1024 * 1024


def _rms(x, g):
    return x * lax.rsqrt(jnp.mean(x * x, axis=-1, keepdims=True) + EPS) * g


def _dot(a, b):
    return jnp.dot(a, b, preferred_element_type=F32)


def _dot_row_halves(a, b):
    half = a.shape[0] // 2
    return jnp.concatenate([_dot(a[:half], b), _dot(a[half:], b)], axis=0)


def _cmul(ar, ai, br, bi):
    return ar * br - ai * bi, ar * bi + ai * br


def _put(ref, val):
    for c in range(ref.shape[0]):
        ref[c] = val[:, c * LANES:(c + 1) * LANES]


def _get_strided(ref, start, n, stride):
    return jnp.concatenate(
        [ref[c, pl.ds(start, n, stride=stride), :] for c in range(ref.shape[0])], axis=1)


def _weight_jobs(layer, wg_hbm, wu_hbm, wd_hbm, wg_v, wu_v, wd_v, stage_in, stage_out, sem):
    jobs = []
    used = [0, 0]

    def add(src, dst, stage, kind, r0, n):
        slot = used[kind] % STAGE_SLOTS
        used[kind] += 1
        copy = pltpu.make_async_copy(src.at[layer, pl.ds(r0, n), :], stage.at[slot], sem.at[kind, slot])
        jobs.append((copy, stage.at[slot], dst, r0, n))

    for r0 in range(0, D_MODEL, STAGE_IN_ROWS):
        add(wg_hbm, wg_v, stage_in, 0, r0, STAGE_IN_ROWS)
        add(wu_hbm, wu_v, stage_in, 0, r0, STAGE_IN_ROWS)
    for r0 in range(0, D_FF, STAGE_OUT_ROWS):
        add(wd_hbm, wd_v, stage_out, 1, r0, STAGE_OUT_ROWS)
    return jobs


def _ffn_kernel(xp_ref, xs_ref, g_ref, wg_hbm, wu_hbm, wd_hbm, gf_ref, op_ref, os_ref,
                wg_ref, wu_ref, wd_ref, stage_in, stage_out, sem, *, layer, final_norm, prompt_tiles):
    i = pl.program_id(0)

    @pl.when(i == 0)
    def _():
        jobs = _weight_jobs(layer, wg_hbm, wu_hbm, wd_hbm, wg_ref, wu_ref, wd_ref, stage_in, stage_out, sem)
        ahead = STAGE_SLOTS - 1
        for k in range(min(ahead, len(jobs))):
            jobs[k][0].start()
        for k, (copy, staged, dst, r0, n) in enumerate(jobs):
            if k + ahead < len(jobs):
                jobs[k + ahead][0].start()
            copy.wait()
            dst[r0:r0 + n, :] = staged[...].astype(BF16)

    x = jnp.where(i < prompt_tiles, xp_ref[...], xs_ref[...])
    h = _rms(x, g_ref[...]).astype(BF16)
    acc = jnp.zeros(x.shape, F32)
    for c in range(D_FF // FFN_COLS):
        sl = slice(c * FFN_COLS, (c + 1) * FFN_COLS)
        g = _dot(h, wg_ref[:, sl])
        u = _dot(h, wu_ref[:, sl])
        a = (g * jax.nn.sigmoid(g) * u).astype(BF16)
        acc = acc + _dot(a, wd_ref[sl, :])
    y = x + 0.5 * acc
    if final_norm:
        y = _rms(y, gf_ref[...])

    @pl.when(i < prompt_tiles)
    def _():
        op_ref[...] = y

    @pl.when(i >= prompt_tiles)
    def _():
        os_ref[...] = y


def _ffn(xp, xs, layer, g, wg, wu, wd, gf, final_norm):
    prompt_tiles = xp.shape[0] // FFN_ROWS
    assert xs.shape[0] == FFN_ROWS
    const = lambda i: (0, 0)
    prompt_map = lambda i: (jnp.minimum(i, prompt_tiles - 1), 0)
    hbm = pl.BlockSpec(memory_space=pl.ANY)
    return pl.pallas_call(
        functools.partial(_ffn_kernel, layer=layer, final_norm=final_norm, prompt_tiles=prompt_tiles),
        grid=(prompt_tiles + 1,),
        in_specs=[
            pl.BlockSpec((FFN_ROWS, D_MODEL), prompt_map),
            pl.BlockSpec((FFN_ROWS, D_MODEL), const),
            pl.BlockSpec((1, D_MODEL), const),
            hbm, hbm, hbm,
            pl.BlockSpec((1, D_MODEL), const),
        ],
        out_specs=[pl.BlockSpec((FFN_ROWS, D_MODEL), prompt_map), pl.BlockSpec((FFN_ROWS, D_MODEL), const)],
        out_shape=[jax.ShapeDtypeStruct(xp.shape, F32), jax.ShapeDtypeStruct(xs.shape, F32)],
        scratch_shapes=[
            pltpu.VMEM((D_MODEL, D_FF), BF16), pltpu.VMEM((D_MODEL, D_FF), BF16), pltpu.VMEM((D_FF, D_MODEL), BF16),
            pltpu.VMEM((STAGE_SLOTS, STAGE_IN_ROWS, D_FF), F32),
            pltpu.VMEM((STAGE_SLOTS, STAGE_OUT_ROWS, D_MODEL), F32),
            pltpu.SemaphoreType.DMA((2, STAGE_SLOTS)),
        ],
        compiler_params=pltpu.CompilerParams(
            dimension_semantics=("arbitrary",), vmem_limit_bytes=VMEM_LIMIT),
        name="ffn",
    )(xp, xs, g, wg, wu, wd, gf)


def _gelu(x):
    z = math.sqrt(2.0 / math.pi) * (x + 0.044715 * (x * x * x))
    return x * jax.nn.sigmoid(2.0 * z)


def _log_sigmoid(x):
    return jnp.minimum(x, 0.0) - jnp.log1p(jnp.exp(-jnp.abs(x)))


def _s5_post(y, u, d, wglu, g):
    z = _gelu(y + d * u)
    out = z * jax.nn.sigmoid(_dot_row_halves(z.astype(BF16), wglu))
    return _rms(out, g)


def _lru_gates(xc, wax, b_a, b_x, lam):
    ga = _dot(xc.astype(BF16), wax)
    r = jax.nn.sigmoid(ga[:, :LRU_WIDTH] + b_a)
    i = jax.nn.sigmoid(ga[:, LRU_WIDTH:] + b_x)
    log_a = LRU_C * r * _log_sigmoid(lam)
    a = jnp.exp(log_a)
    t = jnp.tanh(log_a)
    mult = jnp.sqrt(-2.0 * t / (1.0 - t))
    return a, mult * i * xc


def _sink_softmax(s, sink_col):
    m = jnp.maximum(jnp.max(s, axis=-1, keepdims=True), sink_col)
    e = jnp.exp(s - m)
    den = jnp.sum(e, axis=-1, keepdims=True) + jnp.exp(sink_col - m)
    return e / den


def _mix_out(x, ys, ya, yl, gout, wout):
    ycat = jnp.concatenate([
        ys,
        _rms(ya, gout[:, SSM_WIDTH:SSM_WIDTH + ATT_WIDTH]),
        _rms(yl, gout[:, SSM_WIDTH + ATT_WIDTH:]),
    ], axis=1).astype(BF16)
    return x + _dot(ycat, wout)


def _mixer_prompt_kernel(
        x_ref, xnext_ref, gmix_ref, win_ref, lbre_ref, lbim_ref, bbig_ref, cbig_ref, d_ref, wglu_ref, sink_ref,
        convw_ref, convb_ref, wax_ref, ba_ref, bx_ref, lam_ref, gout_ref, wout_ref, unperm_ref,
        o_ref, kout_ref, vout_ref, sre_ref, sim_ref, convout_ref, lruout_ref,
        st_re, st_im, st_lru, u_buf, up_buf, xs_buf, hs_buf, xl_buf, a_buf, b_buf, hl_buf, lb_buf,
        qz_buf, k_buf, vt_buf, yt_buf, proj_buf, hnext_buf, *, layer):
    j = pl.program_id(1)
    last = pl.num_programs(1) - 1
    T = MIX_ROWS

    @pl.when(j == 0)
    def _():
        st_re[...] = jnp.zeros_like(st_re)
        st_im[...] = jnp.zeros_like(st_im)
        st_lru[...] = jnp.zeros_like(st_lru)
        xl_buf[0:SUBLANES, :] = jnp.zeros((SUBLANES, LRU_WIDTH), F32)
        k_buf[0:WINDOW, :] = jnp.zeros((WINDOW, KV_WIDTH), BF16)
        vt_buf[0] = jnp.zeros((KV_WIDTH, WINDOW), BF16)

    @pl.when(jnp.logical_and(pl.program_id(0) == 0, j == 0))
    def _():
        proj_buf[...] = _dot(_rms(x_ref[0], gmix_ref[...]).astype(BF16), win_ref[...])

    x = x_ref[0]
    proj = proj_buf[...]
    hnext_buf[...] = _rms(xnext_ref[0], gmix_ref[...]).astype(BF16)
    pending = list(range(0, IN_WIDTH, PROJ_COLS))

    def project_next_block():
        c0 = pending.pop(0)
        proj_buf[:, c0:c0 + PROJ_COLS] = _dot(hnext_buf[...], win_ref[:, c0:c0 + PROJ_COLS])

    u = proj[:, :OFF_Q]
    k = proj[:, OFF_K:OFF_V]
    v = proj[:, OFF_V:OFF_LX]
    xl = proj[:, OFF_LX:OFF_LG]
    gl = proj[:, OFF_LG:]
    gout = gout_ref[...]

    _put(u_buf, u)
    for i in range(SEG_LEN):
        up_buf[i * SEGS:(i + 1) * SEGS, :] = _get_strided(u_buf, i, SEGS, SEG_LEN)
    up = up_buf[...]
    xs_buf[...] = _dot(up.astype(BF16), bbig_ref[...])
    lr = lbre_ref[...]
    li = lbim_ref[...]
    lb_buf[0] = jnp.broadcast_to(lr, (SEGS, SSM_FLAT))
    lb_buf[1] = jnp.broadcast_to(li, (SEGS, SSM_FLAT))

    def s5_two_steps(ii, carry, store):
        hr, hi = carry
        r0 = ii * 2 * SEGS
        lrb = lb_buf[0]
        lib = lb_buf[1]
        new_r, new_i = [], []
        for half in range(2):
            xr = xs_buf[pl.ds(r0 + half * SEGS, SEGS), 0:SSM_FLAT]
            xi = xs_buf[pl.ds(r0 + half * SEGS, SEGS), SSM_FLAT:2 * SSM_FLAT]
            hr, hi = lrb * hr - lib * hi + xr, lrb * hi + lib * hr + xi
            new_r.append(hr)
            new_i.append(hi)
        if store:
            hs_buf[pl.ds(r0, 2 * SEGS), 0:SSM_FLAT] = jnp.concatenate(new_r, axis=0).astype(BF16)
            hs_buf[pl.ds(r0, 2 * SEGS), SSM_FLAT:2 * SSM_FLAT] = jnp.concatenate(new_i, axis=0).astype(BF16)
        return hr, hi

    zero_seg = jnp.zeros((SEGS, SSM_FLAT), F32)
    pairs = SEG_LEN // 2
    state = (zero_seg, zero_seg)
    for ii in range(pairs):
        if ii in (0, pairs // 2):
            project_next_block()
        state = s5_two_steps(ii, state, store=False)
    loc_r, loc_i = state
    pr, pi = lr, li
    for _ in range(int(math.log2(SEG_LEN))):
        pr, pi = _cmul(pr, pi, pr, pi)
    gr, gi = st_re[...], st_im[...]
    rows_r, rows_i = [], []
    for s in range(SEGS):
        rows_r.append(gr)
        rows_i.append(gi)
        ar, ai = _cmul(pr, pi, gr, gi)
        gr = ar + loc_r[s:s + 1, :]
        gi = ai + loc_i[s:s + 1, :]
    st_re[...] = gr
    st_im[...] = gi
    state = (jnp.concatenate(rows_r, axis=0), jnp.concatenate(rows_i, axis=0))
    for ii in range(pairs):
        if ii in (0, pairs // 2):
            project_next_block()
        state = s5_two_steps(ii, state, store=True)
    y = _dot_row_halves(hs_buf[...], cbig_ref[...])
    xl_buf[SUBLANES:SUBLANES + T, :] = xl
    cw = convw_ref[...]
    xc = convb_ref[...] + cw[0:1, :] * xl_buf[SUBLANES - 3:SUBLANES - 3 + T, :]
    for t in range(1, CONV_WIDTH):
        xc = xc + cw[t:t + 1, :] * xl_buf[SUBLANES - 3 + t:SUBLANES - 3 + t + T, :]
    tail = xl_buf[T:T + SUBLANES, :]
    xl_buf[0:SUBLANES, :] = tail
    project_next_block()
    a, b = _lru_gates(xc, wax_ref[...], ba_ref[...], bx_ref[...], lam_ref[...])
    a3 = a.reshape(T // SUBLANES, SUBLANES, LRU_WIDTH)
    b3 = b.reshape(T // SUBLANES, SUBLANES, LRU_WIDTH)
    slab_row = lax.broadcasted_iota(jnp.int32, a3.shape, 1)
    shift = 1
    while shift < SUBLANES:
        keep = slab_row >= shift
        b3 = jnp.where(keep, a3 * pltpu.roll(b3, shift, axis=1) + b3, b3)
        a3 = jnp.where(keep, a3 * pltpu.roll(a3, shift, axis=1), a3)
        shift *= 2
    a_buf[...] = a3.reshape(T, LRU_WIDTH)
    b_buf[...] = b3.reshape(T, LRU_WIDTH)
    project_next_block()

    def lru_slab(i, carry):
        r0 = pl.multiple_of(i * SUBLANES, SUBLANES)
        hh = a_buf[pl.ds(r0, SUBLANES), :] * carry + b_buf[pl.ds(r0, SUBLANES), :]
        hl_buf[pl.ds(r0, SUBLANES), :] = hh
        return jnp.broadcast_to(hh[SUBLANES - 1:SUBLANES, :], (SUBLANES, LRU_WIDTH))

    lru_carry = lax.fori_loop(0, T // SUBLANES, lru_slab,
                              jnp.broadcast_to(st_lru[...], (SUBLANES, LRU_WIDTH)), unroll=True)
    gl_state = lru_carry[0:1, :]
    st_lru[...] = gl_state
    yl = _gelu(gl) * hl_buf[...]

    ys_regrouped = _s5_post(y, up, d_ref[...], wglu_ref[...], gout[:, :SSM_WIDTH]).astype(BF16)
    ys = _dot_row_halves(unperm_ref[...], ys_regrouped)

    lo = lax.broadcasted_iota(jnp.int32, (T, LANES), 1) < HEAD_DIM
    for jp in range(N_HEADS // 2):
        pair = proj[:, OFF_Q + jp * LANES:OFF_Q + (jp + 1) * LANES] * ATT_SCALE
        swapped = pltpu.roll(pair, HEAD_DIM, axis=1)
        if jp < N_HEADS // 4:
            even, odd = jnp.where(lo, pair, 0.0), jnp.where(lo, swapped, 0.0)
        else:
            even, odd = jnp.where(lo, 0.0, swapped), jnp.where(lo, 0.0, pair)
        qz_buf[2 * jp] = even.astype(BF16)
        qz_buf[2 * jp + 1] = odd.astype(BF16)
    k_buf[WINDOW:WINDOW + T, :] = k.astype(BF16)
    vt = v.T.astype(BF16)
    for i in range(T // WINDOW):
        vt_buf[i + 1] = vt[:, i * WINDOW:(i + 1) * WINDOW]

    span = 2 * CHUNK
    group = Q_PER_KV * span
    krow = lax.broadcasted_iota(jnp.int32, (2 * span, group), 0)
    qcol = lax.broadcasted_iota(jnp.int32, (2 * span, group), 1) % span
    band = jnp.logical_or(jnp.logical_and(qcol < CHUNK, krow < WINDOW + CHUNK),
                          jnp.logical_and(qcol >= CHUNK, krow >= CHUNK))
    for m in range(T // span):
        valid = band if m > 0 else jnp.logical_and(band, jnp.logical_or(krow >= WINDOW, j > 0))
        top, bot = (CHUNK if m > 0 else WINDOW), WINDOW + CHUNK
        kwin = k_buf[m * span:(m + 2) * span, :]
        vtwin = jnp.concatenate([vt_buf[m], vt_buf[m + 1]], axis=1)
        for kvh in range(N_KV):
            heads = range(kvh * Q_PER_KV, (kvh + 1) * Q_PER_KV)
            qz = jnp.concatenate([qz_buf[h, m * span:(m + 1) * span, :] for h in heads], axis=0)
            s = lax.dot_general(kwin, qz, (((1,), (1,)), ((), ())), preferred_element_type=F32)
            s = jnp.concatenate([jnp.where(valid[:top], s[:top], NEG_INF), s[top:bot],
                                 jnp.where(valid[bot:], s[bot:], NEG_INF)], axis=0)
            sink_row = jnp.concatenate(
                [jnp.full((1, span), sink_ref[layer * N_HEADS + h], F32) for h in heads], axis=1)
            mx = jnp.maximum(jnp.max(s, axis=0, keepdims=True), sink_row)
            e = jnp.exp(s - mx)
            den = jnp.sum(e, axis=0, keepdims=True) + jnp.exp(sink_row - mx)
            o = _dot(vtwin[kvh * HEAD_DIM:(kvh + 1) * HEAD_DIM, :], e.astype(BF16)) * (1.0 / den)
            for g, h in enumerate(heads):
                yt_buf[h * HEAD_DIM:(h + 1) * HEAD_DIM, m * span:(m + 1) * span] = o[:, g * span:(g + 1) * span]
    k_buf[0:WINDOW, :] = k_buf[T:T + WINDOW, :]
    vt_buf[0] = vt_buf[T // WINDOW]

    o_ref[0] = _mix_out(x, ys, yt_buf[...].T, yl, gout, wout_ref[...])
    while pending:
        project_next_block()

    @pl.when(j == last)
    def _():
        kout_ref[0] = k[T - WINDOW:, :]
        vout_ref[0] = v[T - WINDOW:, :]
        sre_ref[0] = gr
        sim_ref[0] = gi
        convout_ref[0] = tail[SUBLANES - (CONV_WIDTH - 1):, :]
        lruout_ref[0] = gl_state


def _mixer_prompt(x, p, layer):
    bn, L, _ = x.shape
    T = MIX_ROWS
    per_b = lambda shape: pl.BlockSpec((1,) + shape, lambda b, j: (b, 0, 0))
    tiles = L // T

    def next_tile(b, j):
        n = jnp.minimum(b * tiles + j + 1, bn * tiles - 1)
        return (n // tiles, n % tiles, 0)

    in_specs = [
        pl.BlockSpec((1, T, D_MODEL), lambda b, j: (b, j, 0)),
        pl.BlockSpec((1, T, D_MODEL), next_tile),
        *_param_specs(p, layer),
        pl.BlockSpec((T, T), lambda b, j: (0, 0)),
    ]
    out_specs = [
        pl.BlockSpec((1, T, D_MODEL), lambda b, j: (b, j, 0)),
        per_b((WINDOW, KV_WIDTH)), per_b((WINDOW, KV_WIDTH)),
        per_b((1, SSM_FLAT)), per_b((1, SSM_FLAT)),
        per_b((CONV_WIDTH - 1, LRU_WIDTH)), per_b((1, LRU_WIDTH)),
    ]
    out_shape = [
        jax.ShapeDtypeStruct((bn, L, D_MODEL), F32),
        jax.ShapeDtypeStruct((bn, WINDOW, KV_WIDTH), F32), jax.ShapeDtypeStruct((bn, WINDOW, KV_WIDTH), F32),
        jax.ShapeDtypeStruct((bn, 1, SSM_FLAT), F32), jax.ShapeDtypeStruct((bn, 1, SSM_FLAT), F32),
        jax.ShapeDtypeStruct((bn, CONV_WIDTH - 1, LRU_WIDTH), F32), jax.ShapeDtypeStruct((bn, 1, LRU_WIDTH), F32),
    ]
    planes = lambda width: pltpu.VMEM((width // LANES, T, LANES), F32)
    scratch = [
        pltpu.VMEM((1, SSM_FLAT), F32), pltpu.VMEM((1, SSM_FLAT), F32), pltpu.VMEM((1, LRU_WIDTH), F32),
        planes(SSM_WIDTH), pltpu.VMEM((T, SSM_WIDTH), F32),
        pltpu.VMEM((T, 2 * SSM_FLAT), F32), pltpu.VMEM((T, 2 * SSM_FLAT), BF16),
        pltpu.VMEM((T + SUBLANES, LRU_WIDTH), F32),
        pltpu.VMEM((T, LRU_WIDTH), F32), pltpu.VMEM((T, LRU_WIDTH), F32), pltpu.VMEM((T, LRU_WIDTH), F32),
        pltpu.VMEM((2, SEGS, SSM_FLAT), F32),
        pltpu.VMEM((N_HEADS, T, LANES), BF16),
        pltpu.VMEM((T + WINDOW, KV_WIDTH), BF16),
        pltpu.VMEM((T // WINDOW + 1, KV_WIDTH, WINDOW), BF16),
        pltpu.VMEM((ATT_WIDTH, T), F32),
        pltpu.VMEM((T, IN_WIDTH), F32), pltpu.VMEM((T, D_MODEL), BF16),
    ]
    return pl.pallas_call(
        functools.partial(_mixer_prompt_kernel, layer=layer),
        grid=(bn, L // T),
        in_specs=in_specs, out_specs=out_specs, out_shape=out_shape, scratch_shapes=scratch,
        compiler_params=pltpu.CompilerParams(
            dimension_semantics=("arbitrary", "arbitrary"), vmem_limit_bytes=VMEM_LIMIT),
        name="mixer_prompt",
    )(x, x, *[p[name] for name in _PARAM_ORDER], _unpermute_matrix())


def _unpermute_matrix():
    token = jnp.arange(MIX_ROWS)
    source = (token % SEG_LEN) * SEGS + token // SEG_LEN
    return (source[:, None] == jnp.arange(MIX_ROWS)[None, :]).astype(BF16)


def _mixer_sample_kernel(
        x_ref, ck_ref, cv_ref, s0re_ref, s0im_ref, conv0_ref, lru0_ref,
        gmix_ref, win_ref, lbre_ref, lbim_ref, bbig_ref, cbig_ref, d_ref, wglu_ref, sink_ref,
        convw_ref, convb_ref, wax_ref, ba_ref, bx_ref, lam_ref, gout_ref, wout_ref, to_stream_ref, to_time_ref,
        o_ref, kout_ref, vout_ref, sre_ref, sim_ref, convout_ref, lruout_ref,
        xs_buf, xp_buf, a_buf, b_buf, hl_buf, qz_buf, kn_buf, vn_buf, ya_buf, *, layer, nb, ns):
    rows = nb * ns
    x = x_ref[...]
    h = _rms(x, gmix_ref[...]).astype(BF16)
    proj = _dot(h, win_ref[...])
    u = proj[:, :OFF_Q]
    k = proj[:, OFF_K:OFF_V]
    v = proj[:, OFF_V:OFF_LX]
    xl = proj[:, OFF_LX:OFF_LG]
    gl = proj[:, OFF_LG:]
    gout = gout_ref[...]
    kout_ref[...] = k
    vout_ref[...] = v

    xs_buf[...] = _dot(u.astype(BF16), bbig_ref[...])
    lr = lbre_ref[...]
    li = lbim_ref[...]
    hr = s0re_ref[...]
    hi = s0im_ref[...]
    for t in range(ns):
        sl = slice(t * nb, (t + 1) * nb)
        nr = lr * hr - li * hi + xs_buf[sl, 0:SSM_FLAT]
        ni = lr * hi + li * hr + xs_buf[sl, SSM_FLAT:2 * SSM_FLAT]
        hr, hi = nr, ni
        xs_buf[sl, 0:SSM_FLAT] = hr
        xs_buf[sl, SSM_FLAT:2 * SSM_FLAT] = hi
    sre_ref[...] = hr
    sim_ref[...] = hi
    y = _dot_row_halves(xs_buf[...].astype(BF16), cbig_ref[...])
    ys = _s5_post(y, u, d_ref[...], wglu_ref[...], gout[:, :SSM_WIDTH])

    npre = (CONV_WIDTH - 1) * nb
    xp_buf[0:npre, :] = conv0_ref[...]
    xp_buf[npre:npre + rows, :] = xl
    cw = convw_ref[...]
    xc = convb_ref[...] + cw[0:1, :] * xp_buf[0:rows, :]
    for t in range(1, CONV_WIDTH):
        xc = xc + cw[t:t + 1, :] * xp_buf[t * nb:t * nb + rows, :]
    convout_ref[...] = xp_buf[rows:rows + npre, :]
    a, b = _lru_gates(xc, wax_ref[...], ba_ref[...], bx_ref[...], lam_ref[...])
    a_buf[...] = a
    b_buf[...] = b
    hh = lru0_ref[...]
    for t in range(ns):
        sl = slice(t * nb, (t + 1) * nb)
        hh = a_buf[sl, :] * hh + b_buf[sl, :]
        hl_buf[sl, :] = hh
    lruout_ref[...] = hh
    yl = _gelu(gl) * hl_buf[...]

    qkv = jnp.concatenate([proj[:, OFF_Q:OFF_K] * ATT_SCALE, k, v], axis=1).astype(BF16)
    qkv = _dot(to_stream_ref[...], qkv)
    lo = lax.broadcasted_iota(jnp.int32, (rows, LANES), 1) < HEAD_DIM
    for jp in range(N_HEADS // 2):
        pair = qkv[:, jp * LANES:(jp + 1) * LANES]
        swapped = pltpu.roll(pair, HEAD_DIM, axis=1)
        if jp < N_HEADS // 4:
            even, odd = jnp.where(lo, pair, 0.0), jnp.where(lo, swapped, 0.0)
        else:
            even, odd = jnp.where(lo, 0.0, swapped), jnp.where(lo, 0.0, pair)
        qz_buf[2 * jp] = even.astype(BF16)
        qz_buf[2 * jp + 1] = odd.astype(BF16)
    kn_buf[...] = qkv[:, ATT_WIDTH:ATT_WIDTH + KV_WIDTH].astype(BF16)
    vn_buf[...] = qkv[:, ATT_WIDTH + KV_WIDTH:].astype(BF16)
    sink_col = jnp.concatenate(
        [jnp.full((ns, 1), sink_ref[layer * N_HEADS + h], F32) for h in range(N_HEADS)], axis=0)
    lo_q = lax.broadcasted_iota(jnp.int32, (ns, LANES), 1) < HEAD_DIM

    def attn_stream(bi, _):
        r0 = pl.multiple_of(bi * ns, ns)
        qz = jnp.concatenate([qz_buf[h, pl.ds(r0, ns), :] for h in range(N_HEADS)], axis=0)
        kk = jnp.concatenate([ck_ref[bi].astype(BF16), kn_buf[pl.ds(r0, ns), :]], axis=0)
        vv = jnp.concatenate([cv_ref[bi].astype(BF16), vn_buf[pl.ds(r0, ns), :]], axis=0)
        s = lax.dot_general(qz, kk, (((1,), (1,)), ((), ())), preferred_element_type=F32)
        o = _dot(_sink_softmax(s, sink_col).astype(BF16), vv)
        pieces = []
        for jp in range(N_HEADS // 2):
            oa = o[2 * jp * ns:(2 * jp + 1) * ns, :]
            ob = o[(2 * jp + 1) * ns:(2 * jp + 2) * ns, :]
            if jp < N_HEADS // 4:
                pieces.append(jnp.where(lo_q, oa, pltpu.roll(ob, HEAD_DIM, axis=1)))
            else:
                pieces.append(jnp.where(lo_q, pltpu.roll(oa, HEAD_DIM, axis=1), ob))
        ya_buf[pl.ds(r0, ns), :] = jnp.concatenate(pieces, axis=1)
        return 0

    lax.fori_loop(0, nb, attn_stream, 0, unroll=2)
    ya = _rms(ya_buf[...], gout[:, SSM_WIDTH:SSM_WIDTH + ATT_WIDTH]).astype(BF16)
    ycat = jnp.concatenate([ys, _dot(to_time_ref[...], ya), _rms(yl, gout[:, SSM_WIDTH + ATT_WIDTH:])],
                           axis=1).astype(BF16)
    o_ref[...] = x + _dot(ycat, wout_ref[...])


def _mixer_sample(x, states, p, layer, nb, ns):
    rows = nb * ns
    whole = lambda shape: pl.BlockSpec(shape, lambda i: (0,) * len(shape))
    in_specs = ([whole(x.shape)] + [_layer_spec(s, layer) for s in states] + _param_specs(p, layer)
                + [whole((rows, rows)), whole((rows, rows))])
    time_row = jnp.arange(rows)
    stream_row = (time_row % nb) * ns + time_row // nb
    to_stream = (jnp.arange(rows)[:, None] == stream_row[None, :]).astype(BF16)
    out_shape = [
        jax.ShapeDtypeStruct((rows, D_MODEL), F32),
        jax.ShapeDtypeStruct((rows, KV_WIDTH), F32), jax.ShapeDtypeStruct((rows, KV_WIDTH), F32),
        jax.ShapeDtypeStruct((nb, SSM_FLAT), F32), jax.ShapeDtypeStruct((nb, SSM_FLAT), F32),
        jax.ShapeDtypeStruct(((CONV_WIDTH - 1) * nb, LRU_WIDTH), F32), jax.ShapeDtypeStruct((nb, LRU_WIDTH), F32),
    ]
    scratch = [
        pltpu.VMEM((rows, 2 * SSM_FLAT), F32),
        pltpu.VMEM((rows + (CONV_WIDTH - 1) * nb, LRU_WIDTH), F32),
        pltpu.VMEM((rows, LRU_WIDTH), F32), pltpu.VMEM((rows, LRU_WIDTH), F32), pltpu.VMEM((rows, LRU_WIDTH), F32),
        pltpu.VMEM((N_HEADS, rows, LANES), BF16),
        pltpu.VMEM((rows, KV_WIDTH), BF16), pltpu.VMEM((rows, KV_WIDTH), BF16),
        pltpu.VMEM((rows, ATT_WIDTH), F32),
    ]
    return pl.pallas_call(
        functools.partial(_mixer_sample_kernel, layer=layer, nb=nb, ns=ns),
        grid=(1,),
        in_specs=in_specs, out_specs=[whole(o.shape) for o in out_shape], out_shape=out_shape,
        scratch_shapes=scratch,
        compiler_params=pltpu.CompilerParams(dimension_semantics=("arbitrary",), vmem_limit_bytes=VMEM_LIMIT),
        name="mixer_sample",
    )(x, *states, *[p[name] for name in _PARAM_ORDER], to_stream, to_stream.T)


def _block_diag(w):
    layers, n, i, o = w.shape
    eye = jnp.eye(n, dtype=w.dtype)
    return jnp.einsum("nm,lnio->lnimo", eye, w).reshape(layers, n * i, n * o)


def _mixer_params(mix_norm, w_in, ssm_a_re, ssm_a_im, ssm_log_dt, ssm_b_re, ssm_b_im, ssm_c_re, ssm_c_im,
                  ssm_d, ssm_w_glu, attn_sink, conv_w, conv_b, lru_w_a, lru_b_a, lru_w_x, lru_b_x, lru_lambda,
                  out_norm, w_out):
    dt = jnp.exp(ssm_log_dt)[..., None]
    mag = jnp.exp(ssm_a_re * dt)
    lb_re = mag * jnp.cos(ssm_a_im * dt)
    lb_im = mag * jnp.sin(ssm_a_im * dt)
    den = ssm_a_re * ssm_a_re + ssm_a_im * ssm_a_im
    nr = lb_re - 1.0
    k_re = (nr * ssm_a_re + lb_im * ssm_a_im) / den
    k_im = (lb_im * ssm_a_re - nr * ssm_a_im) / den
    bb_re = k_re[..., None] * ssm_b_re - k_im[..., None] * ssm_b_im
    bb_im = k_re[..., None] * ssm_b_im + k_im[..., None] * ssm_b_re
    swap = lambda m: jnp.transpose(m, (0, 1, 3, 2))
    bbig = jnp.concatenate([_block_diag(swap(bb_re)), _block_diag(swap(bb_im))], axis=2)
    cbig = jnp.concatenate([_block_diag(swap(ssm_c_re)), -_block_diag(swap(ssm_c_im))], axis=1)
    row = lambda m: m.reshape(m.shape[0], 1, -1)
    return {
        "gmix": row(mix_norm), "win": w_in.astype(BF16),
        "lb_re": row(lb_re), "lb_im": row(lb_im),
        "bbig": bbig.astype(BF16), "cbig": cbig.astype(BF16),
        "d": row(ssm_d), "wglu": ssm_w_glu.astype(BF16), "sink": attn_sink.reshape(-1),
        "convw": conv_w, "convb": row(conv_b),
        "wax": jnp.concatenate([_block_diag(lru_w_a), _block_diag(lru_w_x)], axis=2).astype(BF16),
        "b_a": row(lru_b_a), "b_x": row(lru_b_x), "lam": row(lru_lambda),
        "gout": row(out_norm), "wout": w_out.astype(BF16),
    }


_PARAM_ORDER = ("gmix", "win", "lb_re", "lb_im", "bbig", "cbig", "d", "wglu", "sink",
                "convw", "convb", "wax", "b_a", "b_x", "lam", "gout", "wout")


def _layer_spec(arr, layer):
    tail = (0,) * (arr.ndim - 1)
    return pl.BlockSpec((None,) + arr.shape[1:], lambda *_: (layer,) + tail)


def _param_specs(p, layer):
    return [pl.BlockSpec(memory_space=pltpu.SMEM) if name == "sink" else _layer_spec(p[name], layer)
            for name in _PARAM_ORDER]


def kernel(x_prompt, x_sample, cache_k, cache_v, state_ssm_re, state_ssm_im, state_conv, state_lru,
           ffn1_norm, ffn1_w_gate, ffn1_w_up, ffn1_w_down, mix_norm, w_in,
           ssm_a_re, ssm_a_im, ssm_log_dt, ssm_b_re, ssm_b_im, ssm_c_re, ssm_c_im, ssm_d, ssm_w_glu,
           attn_sink, conv_w, conv_b, lru_w_a, lru_b_a, lru_w_x, lru_b_x, lru_lambda,
           out_norm, w_out, ffn2_norm, ffn2_w_gate, ffn2_w_up, ffn2_w_down, final_norm):
    bn, L, _ = x_prompt.shape
    nb, ns, _ = x_sample.shape
    row = lambda vec: vec.reshape(1, -1)
    gfin = row(final_norm)

    xp = x_prompt.reshape(bn * L, D_MODEL)
    xs = jnp.transpose(x_sample, (1, 0, 2)).reshape(ns * nb, D_MODEL)
    prompt_states = [[] for _ in range(6)]
    sample_states = [[] for _ in range(6)]
    p = _mixer_params(mix_norm, w_in, ssm_a_re, ssm_a_im, ssm_log_dt, ssm_b_re, ssm_b_im, ssm_c_re,
                      ssm_c_im, ssm_d, ssm_w_glu, attn_sink, conv_w, conv_b, lru_w_a, lru_b_a, lru_w_x,
                      lru_b_x, lru_lambda, out_norm, w_out)
    states_in = (
        cache_k.reshape(DEPTH, nb, -1, KV_WIDTH), cache_v.reshape(DEPTH, nb, -1, KV_WIDTH),
        state_ssm_re.reshape(DEPTH, nb, SSM_FLAT), state_ssm_im.reshape(DEPTH, nb, SSM_FLAT),
        jnp.transpose(state_conv, (0, 2, 1, 3)).reshape(DEPTH, (CONV_WIDTH - 1) * nb, LRU_WIDTH),
        state_lru)
    for l in range(DEPTH):
        f1 = (l, row(ffn1_norm[l]), ffn1_w_gate, ffn1_w_up, ffn1_w_down)
        f2 = (l, row(ffn2_norm[l]), ffn2_w_gate, ffn2_w_up, ffn2_w_down)
        is_last = l == DEPTH - 1

        xp, xs = _ffn(xp, xs, *f1, gfin, False)
        xp3, kp, vp, srp, sip, cvp, lrp = _mixer_prompt(xp.reshape(bn, L, D_MODEL), p, l)
        xp = xp3.reshape(bn * L, D_MODEL)
        for lst, s in zip(prompt_states, (
                kp.reshape(bn, WINDOW, N_KV, HEAD_DIM), vp.reshape(bn, WINDOW, N_KV, HEAD_DIM),
                srp.reshape(bn, SSM_GROUPS, SSM_STATE), sip.reshape(bn, SSM_GROUPS, SSM_STATE),
                cvp, lrp.reshape(bn, LRU_WIDTH))):
            lst.append(s)

        xs, ks, vs, srs, sis, cvs, lrs = _mixer_sample(xs, states_in, p, l, nb, ns)
        xp, xs = _ffn(xp, xs, *f2, gfin, is_last)
        unflip = lambda m, w: jnp.transpose(m.reshape(-1, nb, w), (1, 0, 2))
        for lst, s in zip(sample_states, (
                unflip(ks, KV_WIDTH).reshape(nb, ns, N_KV, HEAD_DIM),
                unflip(vs, KV_WIDTH).reshape(nb, ns, N_KV, HEAD_DIM),
                srs.reshape(nb, SSM_GROUPS, SSM_STATE), sis.reshape(nb, SSM_GROUPS, SSM_STATE),
                unflip(cvs, LRU_WIDTH), lrs)):
            lst.append(s)

    y_prompt = xp.reshape(bn, L, D_MODEL)
    y_sample = jnp.transpose(xs.reshape(ns, nb, D_MODEL), (1, 0, 2))
    return (y_prompt, y_sample, *[jnp.stack(c, axis=0) for c in prompt_states],
            *[jnp.stack(c, axis=0) for c in sample_states])
```
